```python
import jax
import jax.numpy as jnp
from jax import lax
import numpy as np

D_MODEL = 1024
BATCH = 8
SEQ = 4096
DEPTH = 4

GRID_W = 64
CTX_LEN = 256
N_Q_HEADS = 8
N_KV_HEADS = 2
HEAD_DIM = 64
Q_GROUP = N_Q_HEADS // N_KV_HEADS
WINDOW = 128
BLOCK = 128
ROPE_THETA = 10000.0
POOL_SIZES = (2, 4, 8, 16)
N_POOL_GROUPS = len(POOL_SIZES)
POOL_GROUP_DIM = D_MODEL // 8
POOL_WIDTH = N_POOL_GROUPS * POOL_GROUP_DIM
Q_WIDTH = N_Q_HEADS * HEAD_DIM
KV_WIDTH = N_KV_HEADS * HEAD_DIM
N_BRANCHES = 2
IN_WIDTH = Q_WIDTH + 2 * KV_WIDTH + POOL_WIDTH + N_BRANCHES * D_MODEL
D_FF = 2816
CONV_WIDTH = 3
N_MOD = 6
EPS = 1e-6
NEG_INF = -1e30

kernel_name = 'hybrid_dit_window_gqa_pool_convffn'


def rms_norm(x, g):
    xf = x.astype(jnp.float32)
    y = xf * lax.rsqrt(jnp.mean(xf * xf, axis=-1, keepdims=True) + EPS)
    return (y * g.astype(jnp.float32)).astype(x.dtype)


def modulate(x, g, shift, scale):
    return rms_norm(x, g) * (1 + scale) + shift


def adaln(cond, w_mod_l, b_mod_l):
    m = jax.nn.silu(cond) @ w_mod_l + b_mod_l
    return jnp.split(m, N_MOD, axis=-1)


def head_rms_norm(t, g):
    tf = t.astype(jnp.float32)
    y = tf * lax.rsqrt(jnp.mean(tf * tf, axis=-1, keepdims=True) + EPS)
    return (y * g.astype(jnp.float32)).astype(t.dtype)


def axial_rope_tables(rows):
    row = jnp.repeat(jnp.arange(rows, dtype=jnp.int32), GRID_W).astype(jnp.float32)
    col = jnp.tile(jnp.arange(GRID_W, dtype=jnp.int32), rows).astype(jnp.float32)
    n_freq = HEAD_DIM // 4
    inv = ROPE_THETA ** (-jnp.arange(n_freq, dtype=jnp.float32) / n_freq)
    ang = jnp.concatenate([row[:, None] * inv, col[:, None] * inv], axis=-1)
    return jnp.cos(ang), jnp.sin(ang)


def apply_axial_rope(t, cos, sin):
    B, N, H, _ = t.shape
    n_freq = HEAD_DIM // 4
    tf = t.astype(jnp.float32).reshape(B, N, H, 2, 2, n_freq)
    cb = cos.reshape(N, 2, n_freq)[None, :, None]
    sb = sin.reshape(N, 2, n_freq)[None, :, None]
    t1 = tf[..., 0, :]
    t2 = tf[..., 1, :]
    out = jnp.stack([t1 * cb - t2 * sb, t2 * cb + t1 * sb], axis=-2)
    return out.reshape(B, N, H, HEAD_DIM).astype(t.dtype)


def split_projection(z):
    B, N, _ = z.shape
    cuts = [Q_WIDTH, Q_WIDTH + KV_WIDTH, Q_WIDTH + 2 * KV_WIDTH, Q_WIDTH + 2 * KV_WIDTH + POOL_WIDTH]
    q, k, v, p, gate = jnp.split(z, cuts, axis=-1)
    q = q.reshape(B, N, N_Q_HEADS, HEAD_DIM)
    k = k.reshape(B, N, N_KV_HEADS, HEAD_DIM)
    v = v.reshape(B, N, N_KV_HEADS, HEAD_DIM)
    return q, k, v, p, gate


def windowed_attention(q, k, v, kc, vc, sink):
    B, N = q.shape[:2]
    nb = N // BLOCK
    scale = HEAD_DIM ** -0.5
    qb = q.reshape(B, nb, BLOCK, N_KV_HEADS, Q_GROUP, HEAD_DIM)
    pad = ((0, 0), (BLOCK, BLOCK), (0, 0), (0, 0))
    kp = jnp.pad(k, pad).reshape(B, nb + 2, BLOCK, N_KV_HEADS, HEAD_DIM)
    vp = jnp.pad(v, pad).reshape(B, nb + 2, BLOCK, N_KV_HEADS, HEAD_DIM)
    kb = jnp.concatenate([kp[:, :-2], kp[:, 1:-1], kp[:, 2:]], axis=2)
    vb = jnp.concatenate([vp[:, :-2], vp[:, 1:-1], vp[:, 2:]], axis=2)
    s_loc = jnp.einsum('bnqhgd,bnkhd->bnhgqk', qb, kb, preferred_element_type=jnp.float32) * scale
    s_ctx = jnp.einsum('bnqhgd,bchd->bnhgqc', qb, kc, preferred_element_type=jnp.float32) * scale
    blk = jnp.arange(nb)[:, None, None]
    qi = jnp.arange(BLOCK)[None, :, None]
    kj = jnp.arange(3 * BLOCK)[None, None, :]
    q_pos = blk * BLOCK + qi
    k_pos = (blk - 1) * BLOCK + kj
    valid = (jnp.abs(k_pos - q_pos) <= WINDOW) & (k_pos >= 0) & (k_pos < N)
    s_loc = jnp.where(valid[None, :, None, None], s_loc, NEG_INF)
    sink_b = sink.astype(jnp.float32).reshape(N_KV_HEADS, Q_GROUP)[None, None, :, :, None, None]
    m = jnp.maximum(jnp.maximum(jnp.max(s_loc, axis=-1, keepdims=True),
                                jnp.max(s_ctx, axis=-1, keepdims=True)), sink_b)
    e_loc = jnp.exp(s_loc - m)
    e_ctx = jnp.exp(s_ctx - m)
    denom = jnp.sum(e_loc, axis=-1, keepdims=True) + jnp.sum(e_ctx, axis=-1, keepdims=True) + jnp.exp(sink_b - m)
    o = (jnp.einsum('bnhgqk,bnkhd->bnhgqd', e_loc, vb.astype(jnp.float32))
         + jnp.einsum('bnhgqc,bchd->bnhgqd', e_ctx, vc.astype(jnp.float32))) / denom
    o = jnp.transpose(o, (0, 1, 4, 2, 3, 5))
    return o.reshape(B, N, Q_WIDTH).astype(q.dtype)


def context_attention(qc, kc, vc, sink):
    B, C = qc.shape[:2]
    scale = HEAD_DIM ** -0.5
    qg = qc.reshape(B, C, N_KV_HEADS, Q_GROUP, HEAD_DIM)
    s = jnp.einsum('bqhgd,bkhd->bhgqk', qg, kc, preferred_element_type=jnp.float32) * scale
    sink_b = jnp.broadcast_to(sink.astype(jnp.float32).reshape(N_KV_HEADS, Q_GROUP)[None, :, :, None, None],
                              s.shape[:-1] + (1,))
    p = jax.nn.softmax(jnp.concatenate([s, sink_b], axis=-1), axis=-1)[..., :-1]
    o = jnp.einsum('bhgqk,bkhd->bqhgd', p, vc.astype(jnp.float32))
    return o.reshape(B, C, Q_WIDTH).astype(qc.dtype)


def multiscale_pool(p):
    B, N, _ = p.shape
    pf = p.astype(jnp.float32)
    cs = jnp.concatenate([jnp.zeros((B, 1, POOL_WIDTH), jnp.float32), jnp.cumsum(pf, axis=1)], axis=1)
    pos = jnp.arange(N)
    outs = []
    for gi, w in enumerate(POOL_SIZES):
        cs_g = cs[..., gi * POOL_GROUP_DIM:(gi + 1) * POOL_GROUP_DIM]
        lo = jnp.clip(pos - w // 2, 0, N)
        hi = jnp.clip(pos + (w - w // 2), 0, N)
        s = jnp.take(cs_g, hi, axis=1) - jnp.take(cs_g, lo, axis=1)
        outs.append(s / (hi - lo).astype(jnp.float32)[None, :, None])
    pooled = jnp.concatenate(outs, axis=-1)
    return (pooled - pf).astype(p.dtype)


def pool_branch(p, w_pool_l, pool_scale_l):
    B, N, _ = p.shape
    d = multiscale_pool(p).reshape(B, N, N_POOL_GROUPS, POOL_GROUP_DIM)
    y = jnp.einsum('bngc,gcd->bngd', d, w_pool_l).reshape(B, N, POOL_WIDTH)
    return y * pool_scale_l


def merge_branches(attn, pool_out, gate, w_br_attn_l, w_br_pool_l, w_out_l):
    g = jax.nn.sigmoid(gate.astype(jnp.float32)).astype(attn.dtype)
    g_attn, g_pool = jnp.split(g, N_BRANCHES, axis=-1)
    y = g_attn * (attn @ w_br_attn_l) + g_pool * (pool_out @ w_br_pool_l)
    return y @ w_out_l


def conv_ffn(h, w_up_l, conv_w_l, conv_b_l, w_down_l):
    N = h.shape[1]
    u = h @ w_up_l
    half = CONV_WIDTH // 2
    up = jnp.pad(u, ((0, 0), (half, half), (0, 0)))
    uc = conv_b_l + up[:, 0:N] * conv_w_l[0]
    for j in range(1, CONV_WIDTH):
        uc = uc + up[:, j:j + N] * conv_w_l[j]
    a, b = jnp.split(uc, 2, axis=-1)
    return (jax.nn.silu(a) * b) @ w_down_l


def setup_inputs(seed: int = 0) -> dict:
    key = jax.random.key(seed)
    ks = jax.random.split(key, 24)
    f32 = jnp.float32
    D = D_MODEL

    def nrm(k, shape, s):
        return jax.random.normal(k, shape, f32) * s

    return {
        'x': nrm(ks[0], (BATCH, SEQ, D), 1.0),
        'c': nrm(ks[1], (BATCH, D), 1.0),
        'ctx': nrm(ks[2], (BATCH, CTX_LEN, D), 1.0),
        'c_ctx': nrm(ks[3], (D,), 1.0),
        'w_mod': nrm(ks[4], (DEPTH, D, N_MOD * D), 0.5 * D ** -0.5),
        'b_mod': nrm(ks[5], (DEPTH, N_MOD * D), 0.01),
        'norm1_g': 1.0 + nrm(ks[6], (DEPTH, D), 0.02),
        'norm2_g': 1.0 + nrm(ks[7], (DEPTH, D), 0.02),
        'w_in': nrm(ks[8], (DEPTH, D, IN_WIDTH), D ** -0.5),
        'q_gain': 1.0 + nrm(ks[9], (DEPTH, HEAD_DIM), 0.02),
        'k_gain': 1.0 + nrm(ks[10], (DEPTH, HEAD_DIM), 0.02),
        'sink': nrm(ks[11], (DEPTH, N_Q_HEADS), 0.5),
        'w_pool': nrm(ks[12], (DEPTH, N_POOL_GROUPS, POOL_GROUP_DIM, POOL_GROUP_DIM), POOL_GROUP_DIM ** -0.5),
        'pool_scale': 1.0 + nrm(ks[13], (DEPTH, POOL_WIDTH), 0.1),
        'w_br_attn': nrm(ks[14], (DEPTH, Q_WIDTH, D), Q_WIDTH ** -0.5),
        'w_br_pool': nrm(ks[15], (DEPTH, POOL_WIDTH, D), POOL_WIDTH ** -0.5),
        'w_out': nrm(ks[16], (DEPTH, D, D), D ** -0.5),
        'w_up': nrm(ks[17], (DEPTH, D, 2 * D_FF), D ** -0.5),
        'conv_w': nrm(ks[18], (DEPTH, CONV_WIDTH, 2 * D_FF), CONV_WIDTH ** -0.5),
        'conv_b': nrm(ks[19], (DEPTH, 2 * D_FF), 0.01),
        'w_down': nrm(ks[20], (DEPTH, D_FF, D), D_FF ** -0.5),
    }


def reference(x, c, ctx, c_ctx, w_mod, b_mod, norm1_g, norm2_g, w_in, q_gain, k_gain, sink,
              w_pool, pool_scale, w_br_attn, w_br_pool, w_out, w_up, conv_w, conv_b, w_down):
    B, n_tok = x.shape[0], x.shape[1]
    C = ctx.shape[1]
    rows = n_tok // GRID_W
    cos, sin = axial_rope_tables(rows)
    xc = ctx
    for l in range(DEPTH):
        last = l == DEPTH - 1
        sh1, sc1, g1, sh2, sc2, g2 = [t[:, None, :] for t in adaln(c, w_mod[l], b_mod[l])]
        csh1, csc1, cg1, csh2, csc2, cg2 = adaln(c_ctx, w_mod[l], b_mod[l])

        hc = modulate(xc, norm1_g[l], csh1, csc1)
        if last:
            kc, vc = jnp.split(hc @ w_in[l][:, Q_WIDTH:Q_WIDTH + 2 * KV_WIDTH], 2, axis=-1)
            kc = head_rms_norm(kc.reshape(B, C, N_KV_HEADS, HEAD_DIM), k_gain[l])
            vc = vc.reshape(B, C, N_KV_HEADS, HEAD_DIM)
        else:
            qc, kc, vc, pc, gatec = split_projection(hc @ w_in[l])
            qc = head_rms_norm(qc, q_gain[l])
            kc = head_rms_norm(kc, k_gain[l])
            mix_c = merge_branches(context_attention(qc, kc, vc, sink[l]),
                                   pool_branch(pc, w_pool[l], pool_scale[l]),
                                   gatec, w_br_attn[l], w_br_pool[l], w_out[l])
            xc_mid = xc + cg1 * mix_c
            xc_next = xc_mid + cg2 * conv_ffn(modulate(xc_mid, norm2_g[l], csh2, csc2),
                                              w_up[l], conv_w[l], conv_b[l], w_down[l])

        h = modulate(x, norm1_g[l], sh1, sc1)
        q, k, v, p, gate = split_projection(h @ w_in[l])
        q = apply_axial_rope(head_rms_norm(q, q_gain[l]), cos, sin)
        k = apply_axial_rope(head_rms_norm(k, k_gain[l]), cos, sin)
        attn = windowed_attention(q, k, v, kc, vc, sink[l])
        pool_out = pool_branch(p, w_pool[l], pool_scale[l])
        x = x + g1 * merge_branches(attn, pool_out, gate, w_br_attn[l], w_br_pool[l], w_out[l])

        x = x + g2 * conv_ffn(modulate(x, norm2_g[l], sh2, sc2), w_up[l], conv_w[l], conv_b[l], w_down[l])

        if not last:
            xc = xc_next
    return x
```

```python
import functools

import jax
import jax.numpy as jnp
from jax import lax
from jax.experimental import pallas as pl
from jax.experimental.pallas import tpu as pltpu

D_MODEL = 1024
DEPTH = 4
GRID_W = 64
N_Q_HEADS = 8
N_KV_HEADS = 2
HEAD_DIM = 64
Q_GROUP = N_Q_HEADS // N_KV_HEADS
WINDOW = 128
BLOCK = 128
ROPE_THETA = 10000.0
POOL_SIZES = (2, 4, 8, 16)
N_POOL_GROUPS = len(POOL_SIZES)
POOL_GROUP_DIM = D_MODEL // 8
POOL_WIDTH = N_POOL_GROUPS * POOL_GROUP_DIM
Q_WIDTH = N_Q_HEADS * HEAD_DIM
KV_WIDTH = N_KV_HEADS * HEAD_DIM
D_FF = 2816
CONV_WIDTH = 3
N_MOD = 6
EPS = 1e-6
NEG_INF = -1e30

LANES = 128
SUBLANES_F32 = 8
SUBLANES_BF16 = 16
VMEM_BYTES_V7X = 64 << 20

KV_POS_WIDTH = 2 * N_KV_HEADS * LANES
PAIR_WIDTH = 2 * HEAD_DIM
POOL_HALO = SUBLANES_F32
CONV_HALO = SUBLANES_BF16
FF_CHUNK = 256
N_FF_CHUNKS = D_FF // FF_CHUNK
MOD_ROWS = 16

F32 = jnp.float32
BF16 = jnp.bfloat16


def _sigmoid(v):
    return 1.0 / (1.0 + jnp.exp(-v))


def _modulate(x, g, shift, scale):
    ms = jnp.mean(x * x, axis=-1, keepdims=True)
    y = x * lax.rsqrt(ms + EPS)
    return (y * g) * (1.0 + scale) + shift


def _dot(a, b):
    return jnp.dot(a, b, preferred_element_type=F32)


def _dot_t(a, b):
    return lax.dot_general(a, b, (((1,), (1,)), ((), ())), preferred_element_type=F32)


def _vmem_limit(pipelined_bytes, resident_bytes, scratch_bytes, temp_bytes):
    need = 2 * pipelined_bytes + 2 * resident_bytes + scratch_bytes + temp_bytes
    assert need < VMEM_BYTES_V7X, need
    return int(need)


def _resident(shape):
    return pl.BlockSpec(shape, lambda *_: (0,) * len(shape))


def _adaln_kernel(c_ref, w_ref, b_ref, o_ref):
    c = c_ref[...]
    s = c * _sigmoid(c)
    o_ref[0] = jnp.dot(s, w_ref[0], preferred_element_type=F32,
                       precision=lax.Precision.HIGHEST) + b_ref[0]


def _adaln(cond, w_mod, b_mod):
    n_out = N_MOD * D_MODEL
    nc = n_out // 4
    return pl.pallas_call(
        _adaln_kernel,
        grid=(DEPTH, n_out // nc),
        in_specs=[
            pl.BlockSpec((MOD_ROWS, D_MODEL), lambda l, j: (0, 0)),
            pl.BlockSpec((1, D_MODEL, nc), lambda l, j: (l, 0, j)),
            pl.BlockSpec((1, 1, nc), lambda l, j: (l, 0, j)),
        ],
        out_specs=pl.BlockSpec((1, MOD_ROWS, nc), lambda l, j: (l, 0, j)),
        out_shape=jax.ShapeDtypeStruct((DEPTH, MOD_ROWS, n_out), F32),
        compiler_params=pltpu.CompilerParams(
            dimension_semantics=("parallel", "parallel"),
            vmem_limit_bytes=_vmem_limit(D_MODEL * nc * 4 + MOD_ROWS * nc * 8, MOD_ROWS * D_MODEL * 4, 0, 4 << 20)),
        name="adaln",
    )(cond, w_mod, b_mod.reshape(DEPTH, 1, n_out))


def _segment_sum_64(v, lane):
    for d in (32, 16, 8, 4, 2, 1):
        partner = jnp.where((lane & d) != 0, pltpu.roll(v, d, 1), pltpu.roll(v, LANES - d, 1))
        v = v + partner
    return v


def _head_norm_rope(z, gain, cos, sin, out_scale):
    lane = lax.broadcasted_iota(jnp.int32, z.shape, 1)
    ss = _segment_sum_64(z * z, lane)
    r = lax.rsqrt(ss * (1.0 / HEAD_DIM) + EPS)
    t = z * gain
    if cos is not None:
        quarter = HEAD_DIM // 4
        rot = jnp.where((lane & quarter) != 0, pltpu.roll(t, quarter, 1), pltpu.roll(t, LANES - quarter, 1))
        t = t * cos + rot * sin
    t = t * r
    if out_scale != 1.0:
        t = t * out_scale
    return t


def _stage_a_kernel(*refs, rope):
    if rope:
        x_ref, mod_ref, g_ref, w_ref, qg_ref, kg_ref, cos_ref, sin_ref, q_ref, kz_ref, vz_ref, p_ref = refs
        cos, sin = cos_ref[...], sin_ref[...]
    else:
        x_ref, mod_ref, g_ref, w_ref, qg_ref, kg_ref, q_ref, kz_ref, vz_ref, p_ref = refs
        cos = sin = None
    d = D_MODEL
    m = mod_ref[0]
    h = _modulate(x_ref[0], g_ref[...], m[:, 0:d], m[:, d:2 * d]).astype(BF16)

    qg, kg = qg_ref[...], kg_ref[...]
    zq = _dot(h, w_ref[:, 0:Q_WIDTH])
    for c in range(Q_WIDTH // LANES):
        sl = slice(c * LANES, (c + 1) * LANES)
        q_ref[0, :, sl] = _head_norm_rope(zq[:, sl], qg, cos, sin, HEAD_DIM ** -0.5).astype(BF16)

    k0 = Q_WIDTH
    v0 = k0 + 2 * KV_WIDTH
    p0 = v0 + 2 * KV_WIDTH
    zk = _dot(h, w_ref[:, k0:v0])
    zv = _dot(h, w_ref[:, v0:p0])
    lane = lax.broadcasted_iota(jnp.int32, (zk.shape[0], LANES), 1)
    low = lane < HEAD_DIM
    for hk in range(N_KV_HEADS):
        sl = slice(hk * LANES, (hk + 1) * LANES)
        kk = _head_norm_rope(zk[:, sl], kg, cos, sin, 1.0)
        vv = zv[:, sl]
        a = slice((2 * hk) * LANES, (2 * hk + 1) * LANES)
        b = slice((2 * hk + 1) * LANES, (2 * hk + 2) * LANES)
        kz_ref[0, :, a] = jnp.where(low, kk, 0.0).astype(BF16)
        kz_ref[0, :, b] = jnp.where(low, 0.0, kk).astype(BF16)
        vz_ref[0, :, a] = jnp.where(low, vv, 0.0).astype(BF16)
        vz_ref[0, :, b] = jnp.where(low, 0.0, vv).astype(BF16)

    p_ref[0] = _dot(h, w_ref[:, p0:p0 + POOL_WIDTH])


def _stage_a(x, mods, mod_row, norm_g, w_a, q_gain, k_gain, rope_tables, *, tm):
    bsz, n, d = x.shape
    wa = w_a.shape[1]
    rope = rope_tables is not None
    in_specs = [
        pl.BlockSpec((1, tm, d), lambda i, b: (b, i, 0)),
        pl.BlockSpec((1, 1, N_MOD * d), lambda i, b: (mod_row(b), 0, 0)),
        _resident((1, d)),
        _resident((d, wa)),
        _resident((1, LANES)),
        _resident((1, LANES)),
    ]
    args = [x, mods, norm_g, w_a, q_gain, k_gain]
    if rope:
        in_specs += [pl.BlockSpec((tm, LANES), lambda i, b: (i, 0))] * 2
        args += list(rope_tables)
    out_shape = [
        jax.ShapeDtypeStruct((bsz, n, Q_WIDTH), BF16),
        jax.ShapeDtypeStruct((bsz, n, KV_POS_WIDTH), BF16),
        jax.ShapeDtypeStruct((bsz, n, KV_POS_WIDTH), BF16),
        jax.ShapeDtypeStruct((bsz, n, POOL_WIDTH), F32),
    ]
    out_specs = [pl.BlockSpec((1, tm, s.shape[2]), lambda i, b: (b, i, 0)) for s in out_shape]
    pipelined = tm * d * 4 + tm * (Q_WIDTH + 2 * KV_POS_WIDTH) * 2 + tm * POOL_WIDTH * 4 + 2 * tm * LANES * 4
    return pl.pallas_call(
        functools.partial(_stage_a_kernel, rope=rope),
        grid=(n // tm, bsz),
        in_specs=in_specs,
        out_specs=out_specs,
        out_shape=out_shape,
        compiler_params=pltpu.CompilerParams(
            dimension_semantics=("parallel", "parallel"),
            vmem_limit_bytes=_vmem_limit(pipelined, d * wa * 2, 0, 6 * tm * d * 4)),
        name="stage_a_rope" if rope else "stage_a",
    )(*args)


def _pool_diff(pbuf, tq, seq_len, tile_start):
    pos = tile_start + lax.broadcasted_iota(jnp.int32, (tq, POOL_GROUP_DIM), 0)
    outs = []
    for gi, w in enumerate(POOL_SIZES):
        cols = slice(gi * POOL_GROUP_DIM, (gi + 1) * POOL_GROUP_DIM)
        lo_off = -(w // 2)
        hi_off = w - w // 2
        acc = pbuf[pl.ds(POOL_HALO + lo_off, tq), cols]
        for j in range(lo_off + 1, hi_off):
            acc = acc + pbuf[pl.ds(POOL_HALO + j, tq), cols]
        cnt = (jnp.minimum(pos + hi_off, seq_len) - jnp.maximum(pos + lo_off, 0)).astype(F32)
        outs.append(acc / cnt - pbuf[pl.ds(POOL_HALO, tq), cols])
    return outs


def _attend(lhs, pieces, sink_a, sink_b):
    def side(idx, sink):
        s = []
        for pc in pieces:
            sc = _dot_t(lhs, pc[idx])
            if pc[4] is not None:
                sc = jnp.where(pc[4], sc, NEG_INF)
            s.append(sc)
        m = sink
        for sc in s:
            m = jnp.maximum(m, jnp.max(sc, axis=-1, keepdims=True))
        e = [jnp.exp(sc - m) for sc in s]
        den = jnp.sum(e[0], axis=-1, keepdims=True)
        for ee in e[1:]:
            den = den + jnp.sum(ee, axis=-1, keepdims=True)
        den = den + jnp.exp(sink - m)
        return e, den

    e_a, den_a = side(0, sink_a)
    e_b, den_b = side(1, sink_b)
    o = None
    for pc, ea, eb in zip(pieces, e_a, e_b):
        t = _dot(ea.astype(BF16), pc[2]) + _dot(eb.astype(BF16), pc[3])
        o = t if o is None else o + t
    lane = lax.broadcasted_iota(jnp.int32, o.shape, 1)
    return o / jnp.where(lane < HEAD_DIM, den_a, den_b)


def _stage_b_kernel(*refs, tq, seq_len, latent):
    if latent:
        (x_ref, mod_ref, g_ref, q_ref, kp_ref, kc_ref, kn_ref, vp_ref, vc_ref, vn_ref, kx_ref, vx_ref,
         pp_ref, pc_ref, pn_ref, sink_ref, wg_ref, wpool_ref, psc_ref, wba_ref, wbp_ref, wo_ref,
         o_ref, kwin, vwin, pbuf, attn) = refs
    else:
        (x_ref, mod_ref, g_ref, q_ref, kc_ref, vc_ref,
         pp_ref, pc_ref, pn_ref, sink_ref, wg_ref, wpool_ref, psc_ref, wba_ref, wbp_ref, wo_ref,
         o_ref, pbuf, attn) = refs
    d = D_MODEL
    i = pl.program_id(1)
    n_tiles = seq_len // tq
    n_qb = tq // BLOCK
    n_blocks = seq_len // BLOCK

    if latent:
        kwin[0:BLOCK] = kp_ref[0]
        kwin[BLOCK:BLOCK + tq] = kc_ref[0]
        kwin[BLOCK + tq:2 * BLOCK + tq] = kn_ref[0]
        vwin[0:BLOCK] = vp_ref[0]
        vwin[BLOCK:BLOCK + tq] = vc_ref[0]
        vwin[BLOCK + tq:2 * BLOCK + tq] = vn_ref[0]
        row = lax.broadcasted_iota(jnp.int32, (2 * BLOCK, 3 * BLOCK), 0) & (BLOCK - 1)
        col = lax.broadcasted_iota(jnp.int32, (2 * BLOCK, 3 * BLOCK), 1)
        band = (col >= row) & (col <= row + 2 * WINDOW)
    srow = lax.broadcasted_iota(jnp.int32, (2 * BLOCK, 1), 0)
    for ib in range(n_qb):
        r0 = ib * BLOCK
        if latent:
            g = i * n_qb + ib
            lo = jnp.where(g > 0, 0, BLOCK)
            hi = jnp.where(g < n_blocks - 1, 3 * BLOCK, 2 * BLOCK)
            mask = band & (col >= lo) & (col < hi)
        for hk in range(N_KV_HEADS):
            q0 = hk * Q_GROUP * HEAD_DIM
            lhs = jnp.concatenate([q_ref[0, r0:r0 + BLOCK, q0:q0 + PAIR_WIDTH],
                                   q_ref[0, r0:r0 + BLOCK, q0 + PAIR_WIDTH:q0 + 2 * PAIR_WIDTH]], axis=0)
            ca = slice((2 * hk) * LANES, (2 * hk + 1) * LANES)
            cb = slice((2 * hk + 1) * LANES, (2 * hk + 2) * LANES)
            if latent:
                rows = slice(r0, r0 + 3 * BLOCK)
                pieces = [(kwin[rows, ca], kwin[rows, cb], vwin[rows, ca], vwin[rows, cb], mask),
                          (kx_ref[0, :, ca], kx_ref[0, :, cb], vx_ref[0, :, ca], vx_ref[0, :, cb], None)]
            else:
                pieces = [(kc_ref[0, :, ca], kc_ref[0, :, cb], vc_ref[0, :, ca], vc_ref[0, :, cb], None)]
            h0 = hk * Q_GROUP
            sink_a = jnp.where(srow < BLOCK, sink_ref[h0], sink_ref[h0 + 2])
            sink_b = jnp.where(srow < BLOCK, sink_ref[h0 + 1], sink_ref[h0 + 3])
            o = _attend(lhs, pieces, sink_a, sink_b).astype(BF16)
            attn[r0:r0 + BLOCK, q0:q0 + PAIR_WIDTH] = o[0:BLOCK]
            attn[r0:r0 + BLOCK, q0 + PAIR_WIDTH:q0 + 2 * PAIR_WIDTH] = o[BLOCK:2 * BLOCK]

    pbuf[0:POOL_HALO] = jnp.where(i > 0, pp_ref[0], 0.0)
    pbuf[POOL_HALO:POOL_HALO + tq] = pc_ref[0]
    pbuf[POOL_HALO + tq:2 * POOL_HALO + tq] = jnp.where(i < n_tiles - 1, pn_ref[0], 0.0)
    pooled = []
    for gi, dg in enumerate(_pool_diff(pbuf, tq, seq_len, i * tq)):
        cols = slice(gi * POOL_GROUP_DIM, (gi + 1) * POOL_GROUP_DIM)
        pooled.append(_dot(dg.astype(BF16), wpool_ref[gi]) * psc_ref[:, cols])
    pool_out = jnp.concatenate(pooled, axis=-1).astype(BF16)

    x = x_ref[0]
    m = mod_ref[0]
    h = _modulate(x, g_ref[...], m[:, 0:d], m[:, d:2 * d]).astype(BF16)
    y = _sigmoid(_dot(h, wg_ref[:, 0:d])) * _dot(attn[...], wba_ref[...])
    y = y + _sigmoid(_dot(h, wg_ref[:, d:2 * d])) * _dot(pool_out, wbp_ref[...])
    o_ref[0] = x + m[:, 2 * d:3 * d] * _dot(y.astype(BF16), wo_ref[...])


def _stage_b(x, mods, mod_row, norm_g, q, kz, vz, ctx_kv, p, sink, w_gate, w_pool, pool_scale,
             w_br_attn, w_br_pool, w_out, *, tq):
    bsz, n, d = x.shape
    latent = ctx_kv is not None
    n_qb = tq // BLOCK
    nb = n // BLOCK
    tile = lambda b, i: (b, i, 0)
    in_specs = [
        pl.BlockSpec((1, tq, d), tile),
        pl.BlockSpec((1, 1, N_MOD * d), lambda b, i: (mod_row(b), 0, 0)),
        _resident((1, d)),
        pl.BlockSpec((1, tq, Q_WIDTH), tile),
    ]
    args = [x, mods, norm_g, q]
    kv_cur = pl.BlockSpec((1, tq, KV_POS_WIDTH), tile)
    if latent:
        kv_prev = pl.BlockSpec((1, BLOCK, KV_POS_WIDTH), lambda b, i: (b, jnp.maximum(i * n_qb - 1, 0), 0))
        kv_next = pl.BlockSpec((1, BLOCK, KV_POS_WIDTH), lambda b, i: (b, jnp.minimum((i + 1) * n_qb, nb - 1), 0))
        n_ctx = ctx_kv[0].shape[1]
        kv_ctx = pl.BlockSpec((1, n_ctx, KV_POS_WIDTH), lambda b, i: (b, 0, 0))
        in_specs += [kv_prev, kv_cur, kv_next, kv_prev, kv_cur, kv_next, kv_ctx, kv_ctx]
        args += [kz, kz, kz, vz, vz, vz, ctx_kv[0], ctx_kv[1]]
    else:
        in_specs += [kv_cur, kv_cur]
        args += [kz, vz]
    n_halo = n // POOL_HALO
    per_tile = tq // POOL_HALO
    in_specs += [
        pl.BlockSpec((1, POOL_HALO, POOL_WIDTH), lambda b, i: (b, jnp.maximum(i * per_tile - 1, 0), 0)),
        pl.BlockSpec((1, tq, POOL_WIDTH), tile),
        pl.BlockSpec((1, POOL_HALO, POOL_WIDTH), lambda b, i: (b, jnp.minimum((i + 1) * per_tile, n_halo - 1), 0)),
        pl.BlockSpec(memory_space=pltpu.SMEM),
        _resident((d, 2 * d)),
        _resident((N_POOL_GROUPS, POOL_GROUP_DIM, POOL_GROUP_DIM)),
        _resident((1, POOL_WIDTH)),
        _resident((Q_WIDTH, d)),
        _resident((POOL_WIDTH, d)),
        _resident((d, d)),
    ]
    args += [p, p, p, sink, w_gate, w_pool, pool_scale, w_br_attn, w_br_pool, w_out]
    scratch = []
    scratch_bytes = 0
    if latent:
        scratch += [pltpu.VMEM((tq + 2 * BLOCK, KV_POS_WIDTH), BF16)] * 2
        scratch_bytes += 2 * (tq + 2 * BLOCK) * KV_POS_WIDTH * 2
    scratch += [pltpu.VMEM((tq + 2 * POOL_HALO, POOL_WIDTH), F32), pltpu.VMEM((tq, Q_WIDTH), BF16)]
    scratch_bytes += (tq + 2 * POOL_HALO) * POOL_WIDTH * 4 + tq * Q_WIDTH * 2
    pipelined = (2 * tq * d * 4 + tq * Q_WIDTH * 2 + 2 * (tq + 3 * BLOCK) * KV_POS_WIDTH * 2
                 + (tq + 2 * POOL_HALO) * POOL_WIDTH * 4)
    resident = (2 * d * d + Q_WIDTH * d + POOL_WIDTH * d + d * d) * 2
    return pl.pallas_call(
        functools.partial(_stage_b_kernel, tq=tq, seq_len=n, latent=latent),
        grid=(bsz, n // tq),
        in_specs=in_specs,
        out_specs=pl.BlockSpec((1, tq, d), tile),
        out_shape=jax.ShapeDtypeStruct((bsz, n, d), F32),
        scratch_shapes=scratch,
        compiler_params=pltpu.CompilerParams(
            dimension_semantics=("parallel", "parallel"),
            vmem_limit_bytes=_vmem_limit(pipelined, resident, scratch_bytes, 12 * tq * d * 4)),
        name="stage_b_latent" if latent else "stage_b_ctx",
    )(*args)


def _stage_c_kernel(x_ref, xp_ref, xn_ref, mod_ref, g_ref, wu_ref, cw_ref, cb_ref, wd_ref, o_ref,
                    hbuf, ubuf, *, tm, seq_len):
    d = D_MODEL
    i = pl.program_id(1)
    n_tiles = seq_len // tm
    m = mod_ref[0]
    g = g_ref[...]
    shift, scale = m[:, 3 * d:4 * d], m[:, 4 * d:5 * d]
    x = x_ref[0]
    hp = jnp.where(i > 0, _modulate(xp_ref[0], g, shift, scale), 0.0)
    hn = jnp.where(i < n_tiles - 1, _modulate(xn_ref[0], g, shift, scale), 0.0)
    hbuf[0:CONV_HALO] = hp.astype(BF16)
    hbuf[CONV_HALO:CONV_HALO + tm] = _modulate(x, g, shift, scale).astype(BF16)
    hbuf[CONV_HALO + tm:2 * CONV_HALO + tm] = hn.astype(BF16)

    acc = None
    for c in range(N_FF_CHUNKS):
        cols = slice(c * 2 * FF_CHUNK, (c + 1) * 2 * FF_CHUNK)
        ubuf[...] = _dot(hbuf[...], wu_ref[:, cols])
        uc = cb_ref[:, cols] + ubuf[pl.ds(CONV_HALO - 1, tm), :] * cw_ref[0:1, cols]
        uc = uc + ubuf[pl.ds(CONV_HALO, tm), :] * cw_ref[1:2, cols]
        uc = uc + ubuf[pl.ds(CONV_HALO + 1, tm), :] * cw_ref[2:3, cols]
        a = uc[:, 0:FF_CHUNK]
        act = ((a * _sigmoid(a)) * uc[:, FF_CHUNK:2 * FF_CHUNK]).astype(BF16)
        t = _dot(act, wd_ref[c * FF_CHUNK:(c + 1) * FF_CHUNK, :])
        acc = t if acc is None else acc + t
    o_ref[0] = x + m[:, 5 * d:6 * d] * acc


def _stage_c(x, mods, mod_row, norm_g, w_up, conv_w, conv_b, w_down, *, tm):
    bsz, n, d = x.shape
    tile = lambda b, i: (b, i, 0)
    per_tile = tm // CONV_HALO
    n_halo = n // CONV_HALO
    in_specs = [
        pl.BlockSpec((1, tm, d), tile),
        pl.BlockSpec((1, CONV_HALO, d), lambda b, i: (b, jnp.maximum(i * per_tile - 1, 0), 0)),
        pl.BlockSpec((1, CONV_HALO, d), lambda b, i: (b, jnp.minimum((i + 1) * per_tile, n_halo - 1), 0)),
        pl.BlockSpec((1, 1, N_MOD * d), lambda b, i: (mod_row(b), 0, 0)),
        _resident((1, d)),
        _resident((d, 2 * D_FF)),
        _resident((CONV_WIDTH, 2 * D_FF)),
        _resident((1, 2 * D_FF)),
        _resident((D_FF, d)),
    ]
    rows = tm + 2 * CONV_HALO
    scratch_bytes = rows * d * 2 + rows * 2 * FF_CHUNK * 4
    return pl.pallas_call(
        functools.partial(_stage_c_kernel, tm=tm, seq_len=n),
        grid=(bsz, n // tm),
        in_specs=in_specs,
        out_specs=pl.BlockSpec((1, tm, d), tile),
        out_shape=jax.ShapeDtypeStruct((bsz, n, d), F32),
        scratch_shapes=[pltpu.VMEM((rows, d), BF16), pltpu.VMEM((rows, 2 * FF_CHUNK), F32)],
        compiler_params=pltpu.CompilerParams(
            dimension_semantics=("parallel", "parallel"),
            vmem_limit_bytes=_vmem_limit(2 * tm * d * 4 + 2 * CONV_HALO * d * 4, 3 * d * D_FF * 2,
                                         scratch_bytes, 5 * tm * d * 4)),
        name="stage_c",
    )(x, x, x, mods, norm_g, w_up, conv_w, conv_b, w_down)


def _rope_tables(n_tok):
    rows = n_tok // GRID_W
    row = jnp.repeat(jnp.arange(rows, dtype=jnp.int32), GRID_W).astype(F32)
    col = jnp.tile(jnp.arange(GRID_W, dtype=jnp.int32), rows).astype(F32)
    n_freq = HEAD_DIM // 4
    inv = ROPE_THETA ** (-jnp.arange(n_freq, dtype=F32) / n_freq)
    ar, ac = row[:, None] * inv, col[:, None] * inv
    cos = jnp.concatenate([jnp.cos(ar), jnp.cos(ar), jnp.cos(ac), jnp.cos(ac)], axis=-1)
    sin = jnp.concatenate([-jnp.sin(ar), jnp.sin(ar), -jnp.sin(ac), jnp.sin(ac)], axis=-1)
    return jnp.tile(cos, (1, LANES // HEAD_DIM)), jnp.tile(sin, (1, LANES // HEAD_DIM))


def _dup_heads(w):
    parts = []
    for hk in range(N_KV_HEADS):
        wh = w[:, hk * HEAD_DIM:(hk + 1) * HEAD_DIM]
        parts += [wh, wh]
    return jnp.concatenate(parts, axis=1)


def _interleave_ff(a):
    parts = []
    for c in range(N_FF_CHUNKS):
        parts += [a[..., c * FF_CHUNK:(c + 1) * FF_CHUNK], a[..., D_FF + c * FF_CHUNK:D_FF + (c + 1) * FF_CHUNK]]
    return jnp.concatenate(parts, axis=-1)


def kernel(x, c, ctx, c_ctx, w_mod, b_mod, norm1_g, norm2_g, w_in, q_gain, k_gain, sink, w_pool, pool_scale,
           w_br_attn, w_br_pool, w_out, w_up, conv_w, conv_b, w_down):
    bsz, n_tok, d = x.shape
    n_ctx = ctx.shape[1]
    assert bsz + 1 <= MOD_ROWS
    cond = jnp.zeros((MOD_ROWS, d), F32).at[:bsz].set(c).at[bsz].set(c_ctx)
    mods_all = _adaln(cond, w_mod, b_mod)
    rope = _rope_tables(n_tok)
    lat_row = lambda b: b
    ctx_row = lambda b: bsz
    tm_lat, tq_lat = 512, 256

    xc = ctx
    for l in range(DEPTH):
        last = l == DEPTH - 1
        mods = mods_all[l].reshape(MOD_ROWS, 1, N_MOD * d)
        wl = w_in[l]
        k0, v0, p0, g0 = Q_WIDTH, Q_WIDTH + KV_WIDTH, Q_WIDTH + 2 * KV_WIDTH, Q_WIDTH + 2 * KV_WIDTH + POOL_WIDTH
        w_a = jnp.concatenate([wl[:, :k0], _dup_heads(wl[:, k0:v0]), _dup_heads(wl[:, v0:p0]), wl[:, p0:g0]],
                              axis=1).astype(BF16)
        w_gate = wl[:, g0:].astype(BF16)
        g1 = norm1_g[l].reshape(1, d)
        g2 = norm2_g[l].reshape(1, d)
        qg = jnp.tile(q_gain[l], LANES // HEAD_DIM).reshape(1, LANES)
        kg = jnp.tile(k_gain[l], LANES // HEAD_DIM).reshape(1, LANES)
        b_weights = (sink[l], w_gate, w_pool[l].astype(BF16), pool_scale[l].reshape(1, POOL_WIDTH),
                     w_br_attn[l].astype(BF16), w_br_pool[l].astype(BF16), w_out[l].astype(BF16))
        c_weights = (g2, _interleave_ff(w_up[l]).astype(BF16), _interleave_ff(conv_w[l]),
                     _interleave_ff(conv_b[l]).reshape(1, 2 * D_FF), w_down[l].astype(BF16))

        qc, kzc, vzc, pc = _stage_a(xc, mods, ctx_row, g1, w_a, qg, kg, None, tm=n_ctx)
        q, kz, vz, p = _stage_a(x, mods, lat_row, g1, w_a, qg, kg, rope, tm=tm_lat)
        x = _stage_b(x, mods, lat_row, g1, q, kz, vz, (kzc, vzc), p, *b_weights, tq=tq_lat)
        x = _stage_c(x, mods, lat_row, *c_weights, tm=tm_lat)
        if not last:
            xc = _stage_b(xc, mods, ctx_row, g1, qc, kzc, vzc, None, pc, *b_weights, tq=n_ctx)
            xc = _stage_c(xc, mods, ctx_row, *c_weights, tm=n_ctx)
    return x
```

```python
import functools

import jax
import jax.numpy as jnp
from jax import lax
from jax.experimental import pallas as pl
from jax.experimental.pallas import tpu as pltpu

D_MODEL = 1024
DEPTH = 4
GRID_W = 64
N_Q_HEADS = 8
N_KV_HEADS = 2
HEAD_DIM = 64
Q_GROUP = N_Q_HEADS // N_KV_HEADS
WINDOW = 128
BLOCK = 128
ROPE_THETA = 10000.0
POOL_SIZES = (2, 4, 8, 16)
N_POOL_GROUPS = len(POOL_SIZES)
POOL_GROUP_DIM = D_MODEL // 8
POOL_WIDTH = N_POOL_GROUPS * POOL_GROUP_DIM
Q_WIDTH = N_Q_HEADS * HEAD_DIM
KV_WIDTH = N_KV_HEADS * HEAD_DIM
D_FF = 2816
CONV_WIDTH = 3
N_MOD = 6
EPS = 1e-6
NEG_INF = -1e30

LANES = 128
SUBLANES_F32 = 8
SUBLANES_BF16 = 16
VMEM_BYTES_V7X = 64 << 20

KV_POS_WIDTH = 2 * N_KV_HEADS * LANES
PAIR_WIDTH = 2 * HEAD_DIM
POOL_HALO = SUBLANES_F32
CONV_HALO = SUBLANES_BF16
FF_CHUNK = 256
N_FF_CHUNKS = D_FF // FF_CHUNK
MOD_ROWS = 16

F32 = jnp.float32
BF16 = jnp.bfloat16


def _sigmoid(v):
    return 1.0 / (1.0 + jnp.exp(-v))


def _modulate(x, g, shift, scale):
    ms = jnp.mean(x * x, axis=-1, keepdims=True)
    y = x * lax.rsqrt(ms + EPS)
    return (y * g) * (1.0 + scale) + shift


def _dot(a, b):
    return jnp.dot(a, b, preferred_element_type=F32)


def _dot_t(a, b):
    return lax.dot_general(a, b, (((1,), (1,)), ((), ())), preferred_element_type=F32)


def _vmem_limit(pipelined_bytes, resident_bytes, scratch_bytes, temp_bytes):
    need = 2 * pipelined_bytes + 2 * resident_bytes + scratch_bytes + temp_bytes
    assert need < VMEM_BYTES_V7X, need
    return int(need)


def _resident(shape):
    return pl.BlockSpec(shape, lambda *_: (0,) * len(shape))


def _adaln_kernel(c_ref, w_ref, b_ref, o_ref):
    c = c_ref[...]
    s = c * _sigmoid(c)
    o_ref[0] = jnp.dot(s, w_ref[0], preferred_element_type=F32,
                       precision=lax.Precision.HIGHEST) + b_ref[0]


def _adaln(cond, w_mod, b_mod):
    n_out = N_MOD * D_MODEL
    nc = n_out // 4
    return pl.pallas_call(
        _adaln_kernel,
        grid=(DEPTH, n_out // nc),
        in_specs=[
            pl.BlockSpec((MOD_ROWS, D_MODEL), lambda l, j: (0, 0)),
            pl.BlockSpec((1, D_MODEL, nc), lambda l, j: (l, 0, j)),
            pl.BlockSpec((1, 1, nc), lambda l, j: (l, 0, j)),
        ],
        out_specs=pl.BlockSpec((1, MOD_ROWS, nc), lambda l, j: (l, 0, j)),
        out_shape=jax.ShapeDtypeStruct((DEPTH, MOD_ROWS, n_out), F32),
        compiler_params=pltpu.CompilerParams(
            dimension_semantics=("parallel", "parallel"),
            vmem_limit_bytes=_vmem_limit(D_MODEL * nc * 4 + MOD_ROWS * nc * 8, MOD_ROWS * D_MODEL * 4, 0, 4 << 20)),
        name="adaln",
    )(cond, w_mod, b_mod.reshape(DEPTH, 1, n_out))


def _head_norm_rope(z, gain, cos, sin, out_scale):
    lane = lax.broadcasted_iota(jnp.int32, z.shape, 1)
    low = lane < HEAD_DIM
    sq = z * z
    ss_lo = jnp.sum(jnp.where(low, sq, 0.0), axis=-1, keepdims=True)
    ss_hi = jnp.sum(jnp.where(low, 0.0, sq), axis=-1, keepdims=True)
    r_lo = lax.rsqrt(ss_lo * (1.0 / HEAD_DIM) + EPS)
    r_hi = lax.rsqrt(ss_hi * (1.0 / HEAD_DIM) + EPS)
    r = jnp.where(low, r_lo, r_hi)
    t = z * gain
    if cos is not None:
        quarter = HEAD_DIM // 4
        rot = jnp.where((lane & quarter) != 0, pltpu.roll(t, quarter, 1), pltpu.roll(t, LANES - quarter, 1))
        t = t * cos + rot * sin
    t = t * r
    if out_scale != 1.0:
        t = t * out_scale
    return t


def _stage_a_kernel(*refs, rope):
    if rope:
        x_ref, mod_ref, g_ref, w_ref, qg_ref, kg_ref, cos_ref, sin_ref, q_ref, kz_ref, vz_ref, p_ref = refs
        cos, sin = cos_ref[...], sin_ref[...]
    else:
        x_ref, mod_ref, g_ref, w_ref, qg_ref, kg_ref, q_ref, kz_ref, vz_ref, p_ref = refs
        cos = sin = None
    d = D_MODEL
    m = mod_ref[0]
    h = _modulate(x_ref[0], g_ref[...], m[:, 0:d], m[:, d:2 * d]).astype(BF16)

    qg, kg = qg_ref[...], kg_ref[...]
    zq = _dot(h, w_ref[:, 0:Q_WIDTH])
    for c in range(Q_WIDTH // LANES):
        sl = slice(c * LANES, (c + 1) * LANES)
        q_ref[0, :, sl] = _head_norm_rope(zq[:, sl], qg, cos, sin, HEAD_DIM ** -0.5).astype(BF16)

    k0 = Q_WIDTH
    v0 = k0 + 2 * KV_WIDTH
    p0 = v0 + 2 * KV_WIDTH
    zk = _dot(h, w_ref[:, k0:v0])
    zv = _dot(h, w_ref[:, v0:p0])
    lane = lax.broadcasted_iota(jnp.int32, (zk.shape[0], LANES), 1)
    low = lane < HEAD_DIM
    for hk in range(N_KV_HEADS):
        sl = slice(hk * LANES, (hk + 1) * LANES)
        kk = _head_norm_rope(zk[:, sl], kg, cos, sin, 1.0)
        vv = zv[:, sl]
        a = slice((2 * hk) * LANES, (2 * hk + 1) * LANES)
        b = slice((2 * hk + 1) * LANES, (2 * hk + 2) * LANES)
        kz_ref[0, :, a] = jnp.where(low, kk, 0.0).astype(BF16)
        kz_ref[0, :, b] = jnp.where(low, 0.0, kk).astype(BF16)
        vz_ref[0, :, a] = jnp.where(low, vv, 0.0).astype(BF16)
        vz_ref[0, :, b] = jnp.where(low, 0.0, vv).astype(BF16)

    p_ref[0] = _dot(h, w_ref[:, p0:p0 + POOL_WIDTH])


def _stage_a(x, mods, mod_row, norm_g, w_a, q_gain, k_gain, rope_tables, *, tm):
    bsz, n, d = x.shape
    wa = w_a.shape[1]
    rope = rope_tables is not None
    in_specs = [
        pl.BlockSpec((1, tm, d), lambda i, b: (b, i, 0)),
        pl.BlockSpec((1, 1, N_MOD * d), lambda i, b: (mod_row(b), 0, 0)),
        _resident((1, d)),
        _resident((d, wa)),
        _resident((1, LANES)),
        _resident((1, LANES)),
    ]
    args = [x, mods, norm_g, w_a, q_gain, k_gain]
    if rope:
        in_specs += [pl.BlockSpec((tm, LANES), lambda i, b: (i, 0))] * 2
        args += list(rope_tables)
    out_shape = [
        jax.ShapeDtypeStruct((bsz, n, Q_WIDTH), BF16),
        jax.ShapeDtypeStruct((bsz, n, KV_POS_WIDTH), BF16),
        jax.ShapeDtypeStruct((bsz, n, KV_POS_WIDTH), BF16),
        jax.ShapeDtypeStruct((bsz, n, POOL_WIDTH), F32),
    ]
    out_specs = [pl.BlockSpec((1, tm, s.shape[2]), lambda i, b: (b, i, 0)) for s in out_shape]
    pipelined = tm * d * 4 + tm * (Q_WIDTH + 2 * KV_POS_WIDTH) * 2 + tm * POOL_WIDTH * 4 + 2 * tm * LANES * 4
    return pl.pallas_call(
        functools.partial(_stage_a_kernel, rope=rope),
        grid=(n // tm, bsz),
        in_specs=in_specs,
        out_specs=out_specs,
        out_shape=out_shape,
        compiler_params=pltpu.CompilerParams(
            dimension_semantics=("parallel", "parallel"),
            vmem_limit_bytes=_vmem_limit(pipelined, d * wa * 2, 0, 6 * tm * d * 4)),
        name="stage_a_rope" if rope else "stage_a",
    )(*args)


def _pool_diff(pbuf, tq, seq_len, tile_start):
    pos = tile_start + lax.broadcasted_iota(jnp.int32, (tq, POOL_GROUP_DIM), 0)
    outs = []
    for gi, w in enumerate(POOL_SIZES):
        cols = slice(gi * POOL_GROUP_DIM, (gi + 1) * POOL_GROUP_DIM)
        lo_off = -(w // 2)
        hi_off = w - w // 2
        acc = pbuf[pl.ds(POOL_HALO + lo_off, tq), cols]
        for j in range(lo_off + 1, hi_off):
            acc = acc + pbuf[pl.ds(POOL_HALO + j, tq), cols]
        cnt = (jnp.minimum(pos + hi_off, seq_len) - jnp.maximum(pos + lo_off, 0)).astype(F32)
        outs.append(acc / cnt - pbuf[pl.ds(POOL_HALO, tq), cols])
    return outs


def _attend(lhs, pieces, sink_a, sink_b):
    def side(idx, sink):
        s = []
        for pc in pieces:
            sc = _dot_t(lhs, pc[idx])
            if pc[4] is not None:
                sc = jnp.where(pc[4], sc, NEG_INF)
            s.append(sc)
        m = sink
        for sc in s:
            m = jnp.maximum(m, jnp.max(sc, axis=-1, keepdims=True))
        e = [jnp.exp(sc - m) for sc in s]
        den = jnp.sum(e[0], axis=-1, keepdims=True)
        for ee in e[1:]:
            den = den + jnp.sum(ee, axis=-1, keepdims=True)
        den = den + jnp.exp(sink - m)
        return e, den

    e_a, den_a = side(0, sink_a)
    e_b, den_b = side(1, sink_b)
    o = None
    for pc, ea, eb in zip(pieces, e_a, e_b):
        t = _dot(ea.astype(BF16), pc[2]) + _dot(eb.astype(BF16), pc[3])
        o = t if o is None else o + t
    lane = lax.broadcasted_iota(jnp.int32, o.shape, 1)
    return o / jnp.where(lane < HEAD_DIM, den_a, den_b)


def _stage_b_kernel(*refs, tq, seq_len, latent):
    if latent:
        (x_ref, mod_ref, g_ref, q_ref, kp_ref, kc_ref, kn_ref, vp_ref, vc_ref, vn_ref, kx_ref, vx_ref,
         pp_ref, pc_ref, pn_ref, sink_ref, wg_ref, wpool_ref, psc_ref, wba_ref, wbp_ref, wo_ref,
         o_ref, kwin, vwin, pbuf, attn) = refs
    else:
        (x_ref, mod_ref, g_ref, q_ref, kc_ref, vc_ref,
         pp_ref, pc_ref, pn_ref, sink_ref, wg_ref, wpool_ref, psc_ref, wba_ref, wbp_ref, wo_ref,
         o_ref, pbuf, attn) = refs
    d = D_MODEL
    i = pl.program_id(1)
    n_tiles = seq_len // tq
    n_qb = tq // BLOCK
    n_blocks = seq_len // BLOCK

    if latent:
        kwin[0:BLOCK] = kp_ref[0]
        kwin[BLOCK:BLOCK + tq] = kc_ref[0]
        kwin[BLOCK + tq:2 * BLOCK + tq] = kn_ref[0]
        vwin[0:BLOCK] = vp_ref[0]
        vwin[BLOCK:BLOCK + tq] = vc_ref[0]
        vwin[BLOCK + tq:2 * BLOCK + tq] = vn_ref[0]
        row = lax.broadcasted_iota(jnp.int32, (2 * BLOCK, 3 * BLOCK), 0) & (BLOCK - 1)
        col = lax.broadcasted_iota(jnp.int32, (2 * BLOCK, 3 * BLOCK), 1)
        band = (col >= row) & (col <= row + 2 * WINDOW)
    srow = lax.broadcasted_iota(jnp.int32, (2 * BLOCK, 1), 0)
    for ib in range(n_qb):
        r0 = ib * BLOCK
        if latent:
            g = i * n_qb + ib
            lo = jnp.where(g > 0, 0, BLOCK)
            hi = jnp.where(g < n_blocks - 1, 3 * BLOCK, 2 * BLOCK)
            mask = band & (col >= lo) & (col < hi)
        for hk in range(N_KV_HEADS):
            q0 = hk * Q_GROUP * HEAD_DIM
            lhs = jnp.concatenate([q_ref[0, r0:r0 + BLOCK, q0:q0 + PAIR_WIDTH],
                                   q_ref[0, r0:r0 + BLOCK, q0 + PAIR_WIDTH:q0 + 2 * PAIR_WIDTH]], axis=0)
            ca = slice((2 * hk) * LANES, (2 * hk + 1) * LANES)
            cb = slice((2 * hk + 1) * LANES, (2 * hk + 2) * LANES)
            if latent:
                rows = slice(r0, r0 + 3 * BLOCK)
                pieces = [(kwin[rows, ca], kwin[rows, cb], vwin[rows, ca], vwin[rows, cb], mask),
                          (kx_ref[0, :, ca], kx_ref[0, :, cb], vx_ref[0, :, ca], vx_ref[0, :, cb], None)]
            else:
                pieces = [(kc_ref[0, :, ca], kc_ref[0, :, cb], vc_ref[0, :, ca], vc_ref[0, :, cb], None)]
            h0 = hk * Q_GROUP
            sink_a = jnp.where(srow < BLOCK, sink_ref[h0], sink_ref[h0 + 2])
            sink_b = jnp.where(srow < BLOCK, sink_ref[h0 + 1], sink_ref[h0 + 3])
            o = _attend(lhs, pieces, sink_a, sink_b).astype(BF16)
            attn[r0:r0 + BLOCK, q0:q0 + PAIR_WIDTH] = o[0:BLOCK]
            attn[r0:r0 + BLOCK, q0 + PAIR_WIDTH:q0 + 2 * PAIR_WIDTH] = o[BLOCK:2 * BLOCK]

    pbuf[0:POOL_HALO] = jnp.where(i > 0, pp_ref[0], 0.0)
    pbuf[POOL_HALO:POOL_HALO + tq] = pc_ref[0]
    pbuf[POOL_HALO + tq:2 * POOL_HALO + tq] = jnp.where(i < n_tiles - 1, pn_ref[0], 0.0)
    pooled = []
    for gi, dg in enumerate(_pool_diff(pbuf, tq, seq_len, i * tq)):
        cols = slice(gi * POOL_GROUP_DIM, (gi + 1) * POOL_GROUP_DIM)
        pooled.append(_dot(dg.astype(BF16), wpool_ref[gi]) * psc_ref[:, cols])
    pool_out = jnp.concatenate(pooled, axis=-1).astype(BF16)

    x = x_ref[0]
    m = mod_ref[0]
    h = _modulate(x, g_ref[...], m[:, 0:d], m[:, d:2 * d]).astype(BF16)
    y = _sigmoid(_dot(h, wg_ref[:, 0:d])) * _dot(attn[...], wba_ref[...])
    y = y + _sigmoid(_dot(h, wg_ref[:, d:2 * d])) * _dot(pool_out, wbp_ref[...])
    o_ref[0] = x + m[:, 2 * d:3 * d] * _dot(y.astype(BF16), wo_ref[...])


def _stage_b(x, mods, mod_row, norm_g, q, kz, vz, ctx_kv, p, sink, w_gate, w_pool, pool_scale,
             w_br_attn, w_br_pool, w_out, *, tq):
    bsz, n, d = x.shape
    latent = ctx_kv is not None
    n_qb = tq // BLOCK
    nb = n // BLOCK
    tile = lambda b, i: (b, i, 0)
    in_specs = [
        pl.BlockSpec((1, tq, d), tile),
        pl.BlockSpec((1, 1, N_MOD * d), lambda b, i: (mod_row(b), 0, 0)),
        _resident((1, d)),
        pl.BlockSpec((1, tq, Q_WIDTH), tile),
    ]
    args = [x, mods, norm_g, q]
    kv_cur = pl.BlockSpec((1, tq, KV_POS_WIDTH), tile)
    if latent:
        kv_prev = pl.BlockSpec((1, BLOCK, KV_POS_WIDTH), lambda b, i: (b, jnp.maximum(i * n_qb - 1, 0), 0))
        kv_next = pl.BlockSpec((1, BLOCK, KV_POS_WIDTH), lambda b, i: (b, jnp.minimum((i + 1) * n_qb, nb - 1), 0))
        n_ctx = ctx_kv[0].shape[1]
        kv_ctx = pl.BlockSpec((1, n_ctx, KV_POS_WIDTH), lambda b, i: (b, 0, 0))
        in_specs += [kv_prev, kv_cur, kv_next, kv_prev, kv_cur, kv_next, kv_ctx, kv_ctx]
        args += [kz, kz, kz, vz, vz, vz, ctx_kv[0], ctx_kv[1]]
    else:
        in_specs += [kv_cur, kv_cur]
        args += [kz, vz]
    n_halo = n // POOL_HALO
    per_tile = tq // POOL_HALO
    in_specs += [
        pl.BlockSpec((1, POOL_HALO, POOL_WIDTH), lambda b, i: (b, jnp.maximum(i * per_tile - 1, 0), 0)),
        pl.BlockSpec((1, tq, POOL_WIDTH), tile),
        pl.BlockSpec((1, POOL_HALO, POOL_WIDTH), lambda b, i: (b, jnp.minimum((i + 1) * per_tile, n_halo - 1), 0)),
        pl.BlockSpec(memory_space=pltpu.SMEM),
        _resident((d, 2 * d)),
        _resident((N_POOL_GROUPS, POOL_GROUP_DIM, POOL_GROUP_DIM)),
        _resident((1, POOL_WIDTH)),
        _resident((Q_WIDTH, d)),
        _resident((POOL_WIDTH, d)),
        _resident((d, d)),
    ]
    args += [p, p, p, sink, w_gate, w_pool, pool_scale, w_br_attn, w_br_pool, w_out]
    scratch = []
    scratch_bytes = 0
    if latent:
        scratch += [pltpu.VMEM((tq + 2 * BLOCK, KV_POS_WIDTH), BF16)] * 2
        scratch_bytes += 2 * (tq + 2 * BLOCK) * KV_POS_WIDTH * 2
    scratch += [pltpu.VMEM((tq + 2 * POOL_HALO, POOL_WIDTH), F32), pltpu.VMEM((tq, Q_WIDTH), BF16)]
    scratch_bytes += (tq + 2 * POOL_HALO) * POOL_WIDTH * 4 + tq * Q_WIDTH * 2
    pipelined = (2 * tq * d * 4 + tq * Q_WIDTH * 2 + 2 * (tq + 3 * BLOCK) * KV_POS_WIDTH * 2
                 + (tq + 2 * POOL_HALO) * POOL_WIDTH * 4)
    resident = (2 * d * d + Q_WIDTH * d + POOL_WIDTH * d + d * d) * 2
    return pl.pallas_call(
        functools.partial(_stage_b_kernel, tq=tq, seq_len=n, latent=latent),
        grid=(bsz, n // tq),
        in_specs=in_specs,
        out_specs=pl.BlockSpec((1, tq, d), tile),
        out_shape=jax.ShapeDtypeStruct((bsz, n, d), F32),
        scratch_shapes=scratch,
        compiler_params=pltpu.CompilerParams(
            dimension_semantics=("parallel", "parallel"),
            vmem_limit_bytes=_vmem_limit(pipelined, resident, scratch_bytes, 12 * tq * d * 4)),
        name="stage_b_latent" if latent else "stage_b_ctx",
    )(*args)


def _stage_c_kernel(x_ref, xp_ref, xn_ref, mod_ref, g_ref, wu_ref, cw_ref, cb_ref, wd_ref, o_ref,
                    hbuf, *, tm, seq_len):
    d = D_MODEL
    i = pl.program_id(1)
    n_tiles = seq_len // tm
    m = mod_ref[0]
    g = g_ref[...]
    shift, scale = m[:, 3 * d:4 * d], m[:, 4 * d:5 * d]
    x = x_ref[0]
    hp = jnp.where(i > 0, _modulate(xp_ref[0], g, shift, scale), 0.0)
    hn = jnp.where(i < n_tiles - 1, _modulate(xn_ref[0], g, shift, scale), 0.0)
    hbuf[0:CONV_HALO] = hp.astype(BF16)
    hbuf[CONV_HALO:CONV_HALO + tm] = _modulate(x, g, shift, scale).astype(BF16)
    hbuf[CONV_HALO + tm:2 * CONV_HALO + tm] = hn.astype(BF16)

    rows = tm + 2 * CONV_HALO
    main = slice(CONV_HALO, CONV_HALO + tm)
    def up(c):
        return _dot(hbuf[...], wu_ref[:, c * 2 * FF_CHUNK:(c + 1) * 2 * FF_CHUNK])

    def conv_act(u, c):
        cols = slice(c * 2 * FF_CHUNK, (c + 1) * 2 * FF_CHUNK)
        uc = cb_ref[:, cols] + pltpu.roll(u, 1, 0)[main] * cw_ref[0:1, cols]
        uc = uc + u[main] * cw_ref[1:2, cols]
        uc = uc + pltpu.roll(u, rows - 1, 0)[main] * cw_ref[2:3, cols]
        a = uc[:, 0:FF_CHUNK]
        return ((a * _sigmoid(a)) * uc[:, FF_CHUNK:2 * FF_CHUNK]).astype(BF16)

    acc = None
    act_prev = None
    ahead = 2
    u_queue = [up(c) for c in range(ahead)]
    for c in range(N_FF_CHUNKS + 1):
        if c + ahead < N_FF_CHUNKS:
            u_queue.append(up(c + ahead))
        act = conv_act(u_queue.pop(0), c) if c < N_FF_CHUNKS else None
        if c >= 1:
            t = _dot(act_prev, wd_ref[(c - 1) * FF_CHUNK:c * FF_CHUNK, :])
            acc = t if acc is None else acc + t
        act_prev = act
    o_ref[0] = x + m[:, 5 * d:6 * d] * acc


def _stage_c(x, mods, mod_row, norm_g, w_up, conv_w, conv_b, w_down, *, tm):
    bsz, n, d = x.shape
    tile = lambda b, i: (b, i, 0)
    per_tile = tm // CONV_HALO
    n_halo = n // CONV_HALO
    in_specs = [
        pl.BlockSpec((1, tm, d), tile),
        pl.BlockSpec((1, CONV_HALO, d), lambda b, i: (b, jnp.maximum(i * per_tile - 1, 0), 0)),
        pl.BlockSpec((1, CONV_HALO, d), lambda b, i: (b, jnp.minimum((i + 1) * per_tile, n_halo - 1), 0)),
        pl.BlockSpec((1, 1, N_MOD * d), lambda b, i: (mod_row(b), 0, 0)),
        _resident((1, d)),
        _resident((d, 2 * D_FF)),
        _resident((CONV_WIDTH, 2 * D_FF)),
        _resident((1, 2 * D_FF)),
        _resident((D_FF, d)),
    ]
    rows = tm + 2 * CONV_HALO
    scratch_bytes = rows * d * 2
    return pl.pallas_call(
        functools.partial(_stage_c_kernel, tm=tm, seq_len=n),
        grid=(bsz, n // tm),
        in_specs=in_specs,
        out_specs=pl.BlockSpec((1, tm, d), tile),
        out_shape=jax.ShapeDtypeStruct((bsz, n, d), F32),
        scratch_shapes=[pltpu.VMEM((rows, d), BF16)],
        compiler_params=pltpu.CompilerParams(
            dimension_semantics=("parallel", "parallel"),
            vmem_limit_bytes=_vmem_limit(2 * tm * d * 4 + 2 * CONV_HALO * d * 4, 3 * d * D_FF * 2,
                                         scratch_bytes, 5 * tm * d * 4)),
        name="stage_c",
    )(x, x, x, mods, norm_g, w_up, conv_w, conv_b, w_down)


def _rope_tables(n_tok):
    rows = n_tok // GRID_W
    row = jnp.repeat(jnp.arange(rows, dtype=jnp.int32), GRID_W).astype(F32)
    col = jnp.tile(jnp.arange(GRID_W, dtype=jnp.int32), rows).astype(F32)
    n_freq = HEAD_DIM // 4
    inv = ROPE_THETA ** (-jnp.arange(n_freq, dtype=F32) / n_freq)
    ar, ac = row[:, None] * inv, col[:, None] * inv
    cos = jnp.concatenate([jnp.cos(ar), jnp.cos(ar), jnp.cos(ac), jnp.cos(ac)], axis=-1)
    sin = jnp.concatenate([-jnp.sin(ar), jnp.sin(ar), -jnp.sin(ac), jnp.sin(ac)], axis=-1)
    return jnp.tile(cos, (1, LANES // HEAD_DIM)), jnp.tile(sin, (1, LANES // HEAD_DIM))


def _dup_heads(w):
    parts = []
    for hk in range(N_KV_HEADS):
        wh = w[:, hk * HEAD_DIM:(hk + 1) * HEAD_DIM]
        parts += [wh, wh]
    return jnp.concatenate(parts, axis=1)


def _interleave_ff(a):
    parts = []
    for c in range(N_FF_CHUNKS):
        parts += [a[..., c * FF_CHUNK:(c + 1) * FF_CHUNK], a[..., D_FF + c * FF_CHUNK:D_FF + (c + 1) * FF_CHUNK]]
    return jnp.concatenate(parts, axis=-1)


def kernel(x, c, ctx, c_ctx, w_mod, b_mod, norm1_g, norm2_g, w_in, q_gain, k_gain, sink, w_pool, pool_scale,
           w_br_attn, w_br_pool, w_out, w_up, conv_w, conv_b, w_down):
    bsz, n_tok, d = x.shape
    n_ctx = ctx.shape[1]
    assert bsz + 1 <= MOD_ROWS
    cond = jnp.zeros((MOD_ROWS, d), F32).at[:bsz].set(c).at[bsz].set(c_ctx)
    mods_all = _adaln(cond, w_mod, b_mod)
    rope = _rope_tables(n_tok)
    lat_row = lambda b: b
    ctx_row = lambda b: bsz
    tm_lat, tq_lat = 512, 256

    xc = ctx
    for l in range(DEPTH):
        last = l == DEPTH - 1
        mods = mods_all[l].reshape(MOD_ROWS, 1, N_MOD * d)
        wl = w_in[l]
        k0, v0, p0, g0 = Q_WIDTH, Q_WIDTH + KV_WIDTH, Q_WIDTH + 2 * KV_WIDTH, Q_WIDTH + 2 * KV_WIDTH + POOL_WIDTH
        w_a = jnp.concatenate([wl[:, :k0], _dup_heads(wl[:, k0:v0]), _dup_heads(wl[:, v0:p0]), wl[:, p0:g0]],
                              axis=1).astype(BF16)
        w_gate = wl[:, g0:].astype(BF16)
        g1 = norm1_g[l].reshape(1, d)
        g2 = norm2_g[l].reshape(1, d)
        qg = jnp.tile(q_gain[l], LANES // HEAD_DIM).reshape(1, LANES)
        kg = jnp.tile(k_gain[l], LANES // HEAD_DIM).reshape(1, LANES)
        b_weights = (sink[l], w_gate, w_pool[l].astype(BF16), pool_scale[l].reshape(1, POOL_WIDTH),
                     w_br_attn[l].astype(BF16), w_br_pool[l].astype(BF16), w_out[l].astype(BF16))
        c_weights = (g2, _interleave_ff(w_up[l]).astype(BF16), _interleave_ff(conv_w[l]),
                     _interleave_ff(conv_b[l]).reshape(1, 2 * D_FF), w_down[l].astype(BF16))

        qc, kzc, vzc, pc = _stage_a(xc, mods, ctx_row, g1, w_a, qg, kg, None, tm=n_ctx)
        q, kz, vz, p = _stage_a(x, mods, lat_row, g1, w_a, qg, kg, rope, tm=tm_lat)
        x = _stage_b(x, mods, lat_row, g1, q, kz, vz, (kzc, vzc), p, *b_weights, tq=tq_lat)
        x = _stage_c(x, mods, lat_row, *c_weights, tm=tm_lat)
        if not last:
            xc = _stage_b(xc, mods, ctx_row, g1, qc, kzc, vzc, None, pc, *b_weights, tq=n_ctx)
            xc = _stage_c(xc, mods, ctx_row, *c_weights, tm=n_ctx)
    return x
```

```python
import functools

import jax
import jax.numpy as jnp
from jax import lax
from jax.experimental import pallas as pl
from jax.experimental.pallas import tpu as pltpu

D_MODEL = 1024
DEPTH = 4
GRID_W = 64
N_Q_HEADS = 8
N_KV_HEADS = 2
HEAD_DIM = 64
Q_GROUP = N_Q_HEADS // N_KV_HEADS
WINDOW = 128
BLOCK = 128
ROPE_THETA = 10000.0
POOL_SIZES = (2, 4, 8, 16)
N_POOL_GROUPS = len(POOL_SIZES)
POOL_GROUP_DIM = D_MODEL // 8
POOL_WIDTH = N_POOL_GROUPS * POOL_GROUP_DIM
Q_WIDTH = N_Q_HEADS * HEAD_DIM
KV_WIDTH = N_KV_HEADS * HEAD_DIM
D_FF = 2816
CONV_WIDTH = 3
N_MOD = 6
EPS = 1e-6
NEG_INF = -1e30

LANES = 128
SUBLANES_F32 = 8
SUBLANES_BF16 = 16
VMEM_BYTES_V7X = 64 << 20

KV_POS_WIDTH = 2 * N_KV_HEADS * LANES
PAIR_WIDTH = 2 * HEAD_DIM
A_WIDTH = Q_WIDTH + 2 * KV_WIDTH + POOL_WIDTH
IN_WIDTH = A_WIDTH + 2 * D_MODEL
POOL_HALO = SUBLANES_F32
CONV_HALO = SUBLANES_F32
UP_AHEAD = 2
FF_CHUNK = 256
N_FF_CHUNKS = D_FF // FF_CHUNK
MOD_ROWS = 16
TM_A, TQ_B, TM_C = 512, 256, 256

F32 = jnp.float32
BF16 = jnp.bfloat16


def _sigmoid(v):
    return 1.0 / (1.0 + jnp.exp(-v))


def _modulate(x, g, shift, scale):
    ms = jnp.mean(x * x, axis=-1, keepdims=True)
    y = x * lax.rsqrt(ms + EPS)
    return (y * g) * (1.0 + scale) + shift


def _dot(a, b):
    return jnp.dot(a, b, preferred_element_type=F32)


def _dot_t(a, b):
    return lax.dot_general(a, b, (((1,), (1,)), ((), ())), preferred_element_type=F32)


def _vmem_limit(pipelined_bytes, resident_bytes, scratch_bytes, temp_bytes):
    need = 2 * pipelined_bytes + 2 * resident_bytes + scratch_bytes + temp_bytes
    assert need < VMEM_BYTES_V7X, need
    return int(need)


def _layer_resident(layer, shape):
    return pl.BlockSpec((None,) + tuple(shape), lambda *_: (layer,) + (0,) * len(shape))


def _adaln_kernel(c_ref, w_ref, b_ref, o_ref):
    c = c_ref[...]
    s = c * _sigmoid(c)
    o_ref[0] = jnp.dot(s, w_ref[0], preferred_element_type=F32,
                       precision=lax.Precision.HIGHEST) + b_ref[0]


def _adaln(cond, w_mod, b_mod):
    n_out = N_MOD * D_MODEL
    nc = n_out // 4
    return pl.pallas_call(
        _adaln_kernel,
        grid=(DEPTH, n_out // nc),
        in_specs=[
            pl.BlockSpec((MOD_ROWS, D_MODEL), lambda l, j: (0, 0)),
            pl.BlockSpec((1, D_MODEL, nc), lambda l, j: (l, 0, j)),
            pl.BlockSpec((1, 1, nc), lambda l, j: (l, 0, j)),
        ],
        out_specs=pl.BlockSpec((1, MOD_ROWS, nc), lambda l, j: (l, 0, j)),
        out_shape=jax.ShapeDtypeStruct((DEPTH, MOD_ROWS, n_out), F32),
        compiler_params=pltpu.CompilerParams(
            dimension_semantics=("parallel", "parallel"),
            vmem_limit_bytes=_vmem_limit(D_MODEL * nc * 4 + MOD_ROWS * nc * 8, MOD_ROWS * D_MODEL * 4, 0, 4 << 20)),
        name="adaln",
    )(cond, w_mod, b_mod.reshape(DEPTH, 1, n_out))


def _head_norm_rope(z, gain, rope, out_scale):
    low = lax.broadcasted_iota(jnp.int32, z.shape, 1) < HEAD_DIM
    sq = z * z
    ss_lo = jnp.sum(jnp.where(low, sq, 0.0), axis=-1, keepdims=True)
    ss_hi = jnp.sum(jnp.where(low, 0.0, sq), axis=-1, keepdims=True)
    r_lo = lax.rsqrt(ss_lo * (1.0 / HEAD_DIM) + EPS)
    r_hi = lax.rsqrt(ss_hi * (1.0 / HEAD_DIM) + EPS)
    r = jnp.where(low, r_lo, r_hi)
    t = z * gain
    if rope is not None:
        z_rot, gain_rot, cos, sin = rope
        t = t * cos + (z_rot * gain_rot) * sin
    t = t * r
    if out_scale != 1.0:
        t = t * out_scale
    return t


def _stage_a_kernel(*refs, rope):
    if rope:
        (x_ref, mod_ref, g_ref, w_ref, qg_ref, kg_ref, wr_ref, qgr_ref, kgr_ref, cos_ref, sin_ref,
         q_ref, kz_ref, vz_ref, p_ref) = refs
        cos, sin = cos_ref[...], sin_ref[...]
    else:
        x_ref, mod_ref, g_ref, w_ref, qg_ref, kg_ref, q_ref, kz_ref, vz_ref, p_ref = refs
    d = D_MODEL
    m = mod_ref[0]
    h = _modulate(x_ref[0], g_ref[...], m[:, 0:d], m[:, d:2 * d]).astype(BF16)

    qg, kg = qg_ref[...], kg_ref[...]
    zq = _dot(h, w_ref[:, 0:Q_WIDTH])
    zq_rot = _dot(h, wr_ref[:, 0:Q_WIDTH]) if rope else None
    for c in range(Q_WIDTH // LANES):
        sl = slice(c * LANES, (c + 1) * LANES)
        rp = (zq_rot[:, sl], qgr_ref[...], cos, sin) if rope else None
        q_ref[0, :, sl] = _head_norm_rope(zq[:, sl], qg, rp, HEAD_DIM ** -0.5).astype(BF16)

    k0 = Q_WIDTH
    p0 = k0 + 2 * KV_WIDTH
    zkv = _dot(h, w_ref[:, k0:p0])
    rp = (_dot(h, wr_ref[:, Q_WIDTH:Q_WIDTH + KV_WIDTH]), kgr_ref[...], cos, sin) if rope else None
    kk = _head_norm_rope(zkv[:, 0:KV_WIDTH], kg, rp, 1.0)
    vv = zkv[:, KV_WIDTH:2 * KV_WIDTH]
    low = lax.broadcasted_iota(jnp.int32, kk.shape, 1) < HEAD_DIM
    for src, dst in ((kk, kz_ref), (vv, vz_ref)):
        swapped = pltpu.roll(src, HEAD_DIM, 1)
        dst[0, :, 0 * LANES:1 * LANES] = jnp.where(low, src, 0.0).astype(BF16)
        dst[0, :, 1 * LANES:2 * LANES] = jnp.where(low, 0.0, swapped).astype(BF16)
        dst[0, :, 2 * LANES:3 * LANES] = jnp.where(low, swapped, 0.0).astype(BF16)
        dst[0, :, 3 * LANES:4 * LANES] = jnp.where(low, 0.0, src).astype(BF16)

    p_ref[0] = _dot(h, w_ref[:, p0:p0 + POOL_WIDTH])


def _stage_a(x, mods, mod_row, layer, norm_g, w_in, q_gain, k_gain, rope_args, *, tm):
    bsz, n, d = x.shape
    wa = A_WIDTH
    rope = rope_args is not None
    in_specs = [
        pl.BlockSpec((1, tm, d), lambda i, b: (b, i, 0)),
        pl.BlockSpec((1, 1, N_MOD * d), lambda i, b: (mod_row(b), 0, 0)),
        _layer_resident(layer, (1, d)),
        _layer_resident(layer, (d, wa)),
        _layer_resident(layer, (1, LANES)),
        _layer_resident(layer, (1, LANES)),
    ]
    args = [x, mods, norm_g, w_in, q_gain, k_gain]
    if rope:
        w_rot, q_gain_rot, k_gain_rot, cos, sin = rope_args
        in_specs += [_layer_resident(layer, (d, Q_WIDTH + KV_WIDTH)), _layer_resident(layer, (1, LANES)),
                     _layer_resident(layer, (1, LANES))]
        in_specs += [pl.BlockSpec((tm, LANES), lambda i, b: (i, 0))] * 2
        args += [w_rot, q_gain_rot, k_gain_rot, cos, sin]
    out_shape = [
        jax.ShapeDtypeStruct((bsz, n, Q_WIDTH), BF16),
        jax.ShapeDtypeStruct((bsz, n, KV_POS_WIDTH), BF16),
        jax.ShapeDtypeStruct((bsz, n, KV_POS_WIDTH), BF16),
        jax.ShapeDtypeStruct((bsz, n, POOL_WIDTH), F32),
    ]
    out_specs = [pl.BlockSpec((1, tm, s.shape[2]), lambda i, b: (b, i, 0)) for s in out_shape]
    pipelined = tm * d * 4 + tm * (Q_WIDTH + 2 * KV_POS_WIDTH) * 2 + tm * POOL_WIDTH * 4 + 2 * tm * LANES * 4
    return pl.pallas_call(
        functools.partial(_stage_a_kernel, rope=rope),
        grid=(n // tm, bsz),
        in_specs=in_specs,
        out_specs=out_specs,
        out_shape=out_shape,
        compiler_params=pltpu.CompilerParams(
            dimension_semantics=("parallel", "parallel"),
            vmem_limit_bytes=_vmem_limit(pipelined, d * (wa + Q_WIDTH + KV_WIDTH) * 2, 0, 8 * tm * d * 4)),
        name="stage_a_rope" if rope else "stage_a",
    )(*args)


def _pool_diff(pbuf, tq, seq_len, tile_start):
    pos = tile_start + lax.broadcasted_iota(jnp.int32, (tq, POOL_GROUP_DIM), 0)
    outs = []
    for gi, w in enumerate(POOL_SIZES):
        cols = slice(gi * POOL_GROUP_DIM, (gi + 1) * POOL_GROUP_DIM)
        lo_off = -(w // 2)
        hi_off = w - w // 2
        acc = pbuf[pl.ds(POOL_HALO + lo_off, tq), cols]
        for j in range(lo_off + 1, hi_off):
            acc = acc + pbuf[pl.ds(POOL_HALO + j, tq), cols]
        cnt = (jnp.minimum(pos + hi_off, seq_len) - jnp.maximum(pos + lo_off, 0)).astype(F32)
        outs.append(acc / cnt - pbuf[pl.ds(POOL_HALO, tq), cols])
    return outs


def _scores(lhs, pieces):
    out = []
    for idx in (0, 1):
        s = []
        for pc in pieces:
            sc = _dot_t(lhs, pc[idx])
            if pc[4] is not None:
                sc = jnp.where(pc[4], sc, NEG_INF)
            s.append(sc)
        out.append(s)
    return out


def _softmax_pv(scores, pieces, sink_a, sink_b):
    def side(s, sink):
        m = sink
        for sc in s:
            m = jnp.maximum(m, jnp.max(sc, axis=-1, keepdims=True))
        e = [jnp.exp(sc - m) for sc in s]
        den = jnp.sum(e[0], axis=-1, keepdims=True)
        for ee in e[1:]:
            den = den + jnp.sum(ee, axis=-1, keepdims=True)
        den = den + jnp.exp(sink - m)
        return e, den

    e_a, den_a = side(scores[0], sink_a)
    e_b, den_b = side(scores[1], sink_b)
    o = None
    for pc, ea, eb in zip(pieces, e_a, e_b):
        t = _dot(ea.astype(BF16), pc[2]) + _dot(eb.astype(BF16), pc[3])
        o = t if o is None else o + t
    lane = lax.broadcasted_iota(jnp.int32, o.shape, 1)
    return o / jnp.where(lane < HEAD_DIM, den_a, den_b)


def _stage_b_kernel(*refs, tq, seq_len, latent, layer):
    if latent:
        (x_ref, mod_ref, g_ref, q_ref, kp_ref, kc_ref, kn_ref, vp_ref, vc_ref, vn_ref, kx_ref, vx_ref,
         pp_ref, pc_ref, pn_ref, sink_ref, wg_ref, wpool_ref, psc_ref, wba_ref, wbp_ref, wo_ref,
         o_ref, kwin, vwin, pbuf, attn) = refs
    else:
        (x_ref, mod_ref, g_ref, q_ref, kc_ref, vc_ref,
         pp_ref, pc_ref, pn_ref, sink_ref, wg_ref, wpool_ref, psc_ref, wba_ref, wbp_ref, wo_ref,
         o_ref, pbuf, attn) = refs
    d = D_MODEL
    i = pl.program_id(1)
    n_tiles = seq_len // tq
    n_qb = tq // BLOCK
    n_blocks = seq_len // BLOCK

    x = x_ref[0]
    m = mod_ref[0]
    h = _modulate(x, g_ref[...], m[:, 0:d], m[:, d:2 * d]).astype(BF16)

    if latent:
        kwin[0:BLOCK] = kp_ref[0]
        kwin[BLOCK:BLOCK + tq] = kc_ref[0]
        kwin[BLOCK + tq:2 * BLOCK + tq] = kn_ref[0]
        vwin[0:BLOCK] = vp_ref[0]
        vwin[BLOCK:BLOCK + tq] = vc_ref[0]
        vwin[BLOCK + tq:2 * BLOCK + tq] = vn_ref[0]
        row = lax.broadcasted_iota(jnp.int32, (2 * BLOCK, 3 * BLOCK), 0) & (BLOCK - 1)
        col = lax.broadcasted_iota(jnp.int32, (2 * BLOCK, 3 * BLOCK), 1)
        band = (col >= row) & (col <= row + 2 * WINDOW)
    srow = lax.broadcasted_iota(jnp.int32, (2 * BLOCK, 1), 0)

    def unit_pieces(ib, hk):
        r0 = ib * BLOCK
        q0 = hk * Q_GROUP * HEAD_DIM
        lhs = jnp.concatenate([q_ref[0, r0:r0 + BLOCK, q0:q0 + PAIR_WIDTH],
                               q_ref[0, r0:r0 + BLOCK, q0 + PAIR_WIDTH:q0 + 2 * PAIR_WIDTH]], axis=0)
        ca = slice((2 * hk) * LANES, (2 * hk + 1) * LANES)
        cb = slice((2 * hk + 1) * LANES, (2 * hk + 2) * LANES)
        if latent:
            g = i * n_qb + ib
            lo = jnp.where(g > 0, 0, BLOCK)
            hi = jnp.where(g < n_blocks - 1, 3 * BLOCK, 2 * BLOCK)
            mask = band & (col >= lo) & (col < hi)
            rows = slice(r0, r0 + 3 * BLOCK)
            pieces = [(kwin[rows, ca], kwin[rows, cb], vwin[rows, ca], vwin[rows, cb], mask),
                      (kx_ref[0, :, ca], kx_ref[0, :, cb], vx_ref[0, :, ca], vx_ref[0, :, cb], None)]
        else:
            pieces = [(kc_ref[0, :, ca], kc_ref[0, :, cb], vc_ref[0, :, ca], vc_ref[0, :, cb], None)]
        return lhs, pieces

    def unit_scores(ib, hk):
        lhs, pieces = unit_pieces(ib, hk)
        return _scores(lhs, pieces), pieces

    def unit_finish(ib, hk, scores, pieces):
        r0 = ib * BLOCK
        q0 = hk * Q_GROUP * HEAD_DIM
        h0 = hk * Q_GROUP
        sink_a = jnp.where(srow < BLOCK, sink_ref[layer, h0], sink_ref[layer, h0 + 2])
        sink_b = jnp.where(srow < BLOCK, sink_ref[layer, h0 + 1], sink_ref[layer, h0 + 3])
        o = _softmax_pv(scores, pieces, sink_a, sink_b).astype(BF16)
        attn[r0:r0 + BLOCK, q0:q0 + PAIR_WIDTH] = o[0:BLOCK]
        attn[r0:r0 + BLOCK, q0 + PAIR_WIDTH:q0 + 2 * PAIR_WIDTH] = o[BLOCK:2 * BLOCK]

    pbuf[0:POOL_HALO] = jnp.where(i > 0, pp_ref[0], 0.0)
    pbuf[POOL_HALO:POOL_HALO + tq] = pc_ref[0]
    pbuf[POOL_HALO + tq:2 * POOL_HALO + tq] = jnp.where(i < n_tiles - 1, pn_ref[0], 0.0)
    pool_d = [dg.astype(BF16) for dg in _pool_diff(pbuf, tq, seq_len, i * tq)]

    def pool_branch():
        pooled = []
        for gi, dg in enumerate(pool_d):
            cols = slice(gi * POOL_GROUP_DIM, (gi + 1) * POOL_GROUP_DIM)
            pooled.append(_dot(dg, wpool_ref[gi]) * psc_ref[:, cols])
        return _dot(jnp.concatenate(pooled, axis=-1).astype(BF16), wbp_ref[...])

    units = [(ib, hk) for ib in range(n_qb) for hk in range(N_KV_HEADS)]
    dense = [
        lambda: _sigmoid(_dot(h, wg_ref[:, A_WIDTH:A_WIDTH + d])),
        lambda: _sigmoid(_dot(h, wg_ref[:, A_WIDTH + d:A_WIDTH + 2 * d])),
        pool_branch,
    ]
    dense_out = []
    pending = unit_scores(*units[0])
    for k, u in enumerate(units):
        nxt = unit_scores(*units[k + 1]) if k + 1 < len(units) else None
        if k < len(dense):
            dense_out.append(dense[k]())
        unit_finish(*u, *pending)
        pending = nxt
    for f in dense[len(dense_out):]:
        dense_out.append(f())
    gate_attn, gate_pool, z_pool = dense_out
    y = gate_attn * _dot(attn[...], wba_ref[...]) + gate_pool * z_pool
    o_ref[0] = x + m[:, 2 * d:3 * d] * _dot(y.astype(BF16), wo_ref[...])


def _stage_b(x, mods, mod_row, layer, norm_g, q, kz, vz, ctx_kv, p, sink, w_in, w_pool, pool_scale,
             w_br_attn, w_br_pool, w_out, *, tq):
    bsz, n, d = x.shape
    latent = ctx_kv is not None
    n_qb = tq // BLOCK
    nb = n // BLOCK
    tile = lambda b, i: (b, i, 0)
    in_specs = [
        pl.BlockSpec((1, tq, d), tile),
        pl.BlockSpec((1, 1, N_MOD * d), lambda b, i: (mod_row(b), 0, 0)),
        _layer_resident(layer, (1, d)),
        pl.BlockSpec((1, tq, Q_WIDTH), tile),
    ]
    args = [x, mods, norm_g, q]
    kv_cur = pl.BlockSpec((1, tq, KV_POS_WIDTH), tile)
    if latent:
        kv_prev = pl.BlockSpec((1, BLOCK, KV_POS_WIDTH), lambda b, i: (b, jnp.maximum(i * n_qb - 1, 0), 0))
        kv_next = pl.BlockSpec((1, BLOCK, KV_POS_WIDTH), lambda b, i: (b, jnp.minimum((i + 1) * n_qb, nb - 1), 0))
        n_ctx = ctx_kv[0].shape[1]
        kv_ctx = pl.BlockSpec((1, n_ctx, KV_POS_WIDTH), lambda b, i: (b, 0, 0))
        in_specs += [kv_prev, kv_cur, kv_next, kv_prev, kv_cur, kv_next, kv_ctx, kv_ctx]
        args += [kz, kz, kz, vz, vz, vz, ctx_kv[0], ctx_kv[1]]
    else:
        in_specs += [kv_cur, kv_cur]
        args += [kz, vz]
    n_halo = n // POOL_HALO
    per_tile = tq // POOL_HALO
    in_specs += [
        pl.BlockSpec((1, POOL_HALO, POOL_WIDTH), lambda b, i: (b, jnp.maximum(i * per_tile - 1, 0), 0)),
        pl.BlockSpec((1, tq, POOL_WIDTH), tile),
        pl.BlockSpec((1, POOL_HALO, POOL_WIDTH), lambda b, i: (b, jnp.minimum((i + 1) * per_tile, n_halo - 1), 0)),
        pl.BlockSpec(memory_space=pltpu.SMEM),
        _layer_resident(layer, (d, IN_WIDTH)),
        _layer_resident(layer, (N_POOL_GROUPS, POOL_GROUP_DIM, POOL_GROUP_DIM)),
        _layer_resident(layer, (1, POOL_WIDTH)),
        _layer_resident(layer, (Q_WIDTH, d)),
        _layer_resident(layer, (POOL_WIDTH, d)),
        _layer_resident(layer, (d, d)),
    ]
    args += [p, p, p, sink, w_in, w_pool, pool_scale, w_br_attn, w_br_pool, w_out]
    scratch = []
    scratch_bytes = 0
    if latent:
        scratch += [pltpu.VMEM((tq + 2 * BLOCK, KV_POS_WIDTH), BF16)] * 2
        scratch_bytes += 2 * (tq + 2 * BLOCK) * KV_POS_WIDTH * 2
    scratch += [pltpu.VMEM((tq + 2 * POOL_HALO, POOL_WIDTH), F32), pltpu.VMEM((tq, Q_WIDTH), BF16)]
    scratch_bytes += (tq + 2 * POOL_HALO) * POOL_WIDTH * 4 + tq * Q_WIDTH * 2
    pipelined = (2 * tq * d * 4 + tq * Q_WIDTH * 2 + 2 * (tq + 3 * BLOCK) * KV_POS_WIDTH * 2
                 + (tq + 2 * POOL_HALO) * POOL_WIDTH * 4)
    resident = (IN_WIDTH * d + Q_WIDTH * d + POOL_WIDTH * d + d * d) * 2
    return pl.pallas_call(
        functools.partial(_stage_b_kernel, tq=tq, seq_len=n, latent=latent, layer=layer),
        grid=(bsz, n // tq),
        in_specs=in_specs,
        out_specs=pl.BlockSpec((1, tq, d), tile),
        out_shape=jax.ShapeDtypeStruct((bsz, n, d), F32),
        scratch_shapes=scratch,
        compiler_params=pltpu.CompilerParams(
            dimension_semantics=("parallel", "parallel"),
            vmem_limit_bytes=_vmem_limit(pipelined, resident, scratch_bytes, 12 * tq * d * 4)),
        name="stage_b_latent" if latent else "stage_b_ctx",
    )(*args)


def _stage_c_kernel(x_ref, xp_ref, xn_ref, mod_ref, g_ref, wu_ref, cw_ref, cb_ref, wd_ref, o_ref,
                    hbuf, *, tm, seq_len):
    d = D_MODEL
    i = pl.program_id(1)
    n_tiles = seq_len // tm
    m = mod_ref[0]
    g = g_ref[...]
    shift, scale = m[:, 3 * d:4 * d], m[:, 4 * d:5 * d]
    x = x_ref[0]
    hn = jnp.where(i < n_tiles - 1, _modulate(xn_ref[0], g, shift, scale), 0.0)
    hp = jnp.where(i > 0, _modulate(xp_ref[0], g, shift, scale), 0.0)
    hbuf[0:tm] = _modulate(x, g, shift, scale).astype(BF16)
    hbuf[tm:tm + 2 * CONV_HALO] = jnp.concatenate([hn, hp], axis=0).astype(BF16)

    rows = tm + 2 * CONV_HALO
    assert tm % SUBLANES_BF16 == 0 and (2 * CONV_HALO) % SUBLANES_BF16 == 0

    def up(c):
        return _dot(hbuf[...], wu_ref[:, c * 2 * FF_CHUNK:(c + 1) * 2 * FF_CHUNK])

    def conv_act(u, c):
        cols = slice(c * 2 * FF_CHUNK, (c + 1) * 2 * FF_CHUNK)
        uc = cb_ref[:, cols] + pltpu.roll(u, 1, 0)[0:tm] * cw_ref[0:1, cols]
        uc = uc + u[0:tm] * cw_ref[1:2, cols]
        uc = uc + pltpu.roll(u, rows - 1, 0)[0:tm] * cw_ref[2:3, cols]
        a = uc[:, 0:FF_CHUNK]
        return ((a * _sigmoid(a)) * uc[:, FF_CHUNK:2 * FF_CHUNK]).astype(BF16)

    acc = None
    act_prev = None
    u_queue = [up(c) for c in range(UP_AHEAD)]
    for c in range(N_FF_CHUNKS + 1):
        if c + UP_AHEAD < N_FF_CHUNKS:
            u_queue.append(up(c + UP_AHEAD))
        act = conv_act(u_queue.pop(0), c) if c < N_FF_CHUNKS else None
        if c >= 1:
            krows = slice((c - 1) * FF_CHUNK, c * FF_CHUNK)
            t = [_dot(act_prev, wd_ref[krows, j * (d // 2):(j + 1) * (d // 2)]) for j in range(2)]
            acc = t if acc is None else [a + b for a, b in zip(acc, t)]
        act_prev = act
    o_ref[0] = x + m[:, 5 * d:6 * d] * jnp.concatenate(acc, axis=-1)


def _stage_c(x, mods, mod_row, layer, norm_g, w_up, conv_w, conv_b, w_down, *, tm):
    bsz, n, d = x.shape
    tile = lambda b, i: (b, i, 0)
    per_tile = tm // CONV_HALO
    n_halo = n // CONV_HALO
    in_specs = [
        pl.BlockSpec((1, tm, d), tile),
        pl.BlockSpec((1, CONV_HALO, d), lambda b, i: (b, jnp.maximum(i * per_tile - 1, 0), 0)),
        pl.BlockSpec((1, CONV_HALO, d), lambda b, i: (b, jnp.minimum((i + 1) * per_tile, n_halo - 1), 0)),
        pl.BlockSpec((1, 1, N_MOD * d), lambda b, i: (mod_row(b), 0, 0)),
        _layer_resident(layer, (1, d)),
        _layer_resident(layer, (d, 2 * D_FF)),
        _layer_resident(layer, (CONV_WIDTH, 2 * D_FF)),
        _layer_resident(layer, (1, 2 * D_FF)),
        _layer_resident(layer, (D_FF, d)),
    ]
    rows = tm + 2 * CONV_HALO
    scratch_bytes = rows * d * 2
    return pl.pallas_call(
        functools.partial(_stage_c_kernel, tm=tm, seq_len=n),
        grid=(bsz, n // tm),
        in_specs=in_specs,
        out_specs=pl.BlockSpec((1, tm, d), tile),
        out_shape=jax.ShapeDtypeStruct((bsz, n, d), F32),
        scratch_shapes=[pltpu.VMEM((rows, d), BF16)],
        compiler_params=pltpu.CompilerParams(
            dimension_semantics=("parallel", "parallel"),
            vmem_limit_bytes=_vmem_limit(2 * tm * d * 4 + 2 * CONV_HALO * d * 4, 3 * d * D_FF * 2,
                                         scratch_bytes, 8 * tm * d * 4)),
        name="stage_c",
    )(x, x, x, mods, norm_g, w_up, conv_w, conv_b, w_down)


def _rope_tables(n_tok):
    rows = n_tok // GRID_W
    row = jnp.repeat(jnp.arange(rows, dtype=jnp.int32), GRID_W).astype(F32)
    col = jnp.tile(jnp.arange(GRID_W, dtype=jnp.int32), rows).astype(F32)
    n_freq = HEAD_DIM // 4
    inv = ROPE_THETA ** (-jnp.arange(n_freq, dtype=F32) / n_freq)
    ar, ac = row[:, None] * inv, col[:, None] * inv
    cos = jnp.concatenate([jnp.cos(ar), jnp.cos(ar), jnp.cos(ac), jnp.cos(ac)], axis=-1)
    sin = jnp.concatenate([-jnp.sin(ar), jnp.sin(ar), -jnp.sin(ac), jnp.sin(ac)], axis=-1)
    return jnp.tile(cos, (1, LANES // HEAD_DIM)), jnp.tile(sin, (1, LANES // HEAD_DIM))


def _swap_rope_halves(a):
    lead = a.shape[:-1]
    quarter = HEAD_DIM // 4
    a = a.reshape(lead + (a.shape[-1] // (2 * quarter), 2, quarter))
    return jnp.flip(a, axis=-2).reshape(lead + (-1,))


def _interleave_ff(a):
    lead = a.shape[:-1]
    a = a.reshape(lead + (2, N_FF_CHUNKS, FF_CHUNK))
    return jnp.swapaxes(a, -3, -2).reshape(lead + (2 * D_FF,))


def kernel(x, c, ctx, c_ctx, w_mod, b_mod, norm1_g, norm2_g, w_in, q_gain, k_gain, sink, w_pool, pool_scale,
           w_br_attn, w_br_pool, w_out, w_up, conv_w, conv_b, w_down):
    bsz, n_tok, d = x.shape
    n_ctx = ctx.shape[1]
    assert bsz + 1 <= MOD_ROWS
    cond = jnp.zeros((MOD_ROWS, d), F32).at[:bsz].set(c).at[bsz].set(c_ctx)
    mods = _adaln(cond, w_mod, b_mod).reshape(DEPTH * MOD_ROWS, 1, N_MOD * d)

    w_in_b = w_in.astype(BF16)
    g1 = norm1_g.reshape(DEPTH, 1, d)
    g2 = norm2_g.reshape(DEPTH, 1, d)
    qg = jnp.tile(q_gain, (1, LANES // HEAD_DIM)).reshape(DEPTH, 1, LANES)
    kg = jnp.tile(k_gain, (1, LANES // HEAD_DIM)).reshape(DEPTH, 1, LANES)
    rope_args = (_swap_rope_halves(w_in_b[:, :, :Q_WIDTH + KV_WIDTH]),
                 _swap_rope_halves(qg), _swap_rope_halves(kg)) + _rope_tables(n_tok)
    b_weights = (sink, w_in_b, w_pool.astype(BF16), pool_scale.reshape(DEPTH, 1, POOL_WIDTH),
                 w_br_attn.astype(BF16), w_br_pool.astype(BF16), w_out.astype(BF16))
    c_weights = (g2, _interleave_ff(w_up).astype(BF16), _interleave_ff(conv_w),
                 _interleave_ff(conv_b).reshape(DEPTH, 1, 2 * D_FF), w_down.astype(BF16))

    xc = ctx
    for l in range(DEPTH):
        last = l == DEPTH - 1
        lat_row = lambda b, l=l: l * MOD_ROWS + b
        ctx_row = lambda b, l=l: l * MOD_ROWS + bsz
        qc, kzc, vzc, pc = _stage_a(xc, mods, ctx_row, l, g1, w_in_b, qg, kg, None, tm=n_ctx)
        q, kz, vz, p = _stage_a(x, mods, lat_row, l, g1, w_in_b, qg, kg, rope_args, tm=TM_A)
        x = _stage_b(x, mods, lat_row, l, g1, q, kz, vz, (kzc, vzc), p, *b_weights, tq=TQ_B)
        x = _stage_c(x, mods, lat_row, l, *c_weights, tm=TM_C)
        if not last:
            xc = _stage_b(xc, mods, ctx_row, l, g1, qc, kzc, vzc, None, pc, *b_weights, tq=n_ctx)
            xc = _stage_c(xc, mods, ctx_row, l, *c_weights, tm=n_ctx)
    return x
```

```python
import functools

import jax
import jax.numpy as jnp
from jax import lax
from jax.experimental import pallas as pl
from jax.experimental.pallas import tpu as pltpu

D_MODEL = 1024
DEPTH = 4
GRID_W = 64
N_Q_HEADS = 8
N_KV_HEADS = 2
HEAD_DIM = 64
Q_GROUP = N_Q_HEADS // N_KV_HEADS
WINDOW = 128
BLOCK = 128
ROPE_THETA = 10000.0
POOL_SIZES = (2, 4, 8, 16)
N_POOL_GROUPS = len(POOL_SIZES)
POOL_GROUP_DIM = D_MODEL // 8
POOL_WIDTH = N_POOL_GROUPS * POOL_GROUP_DIM
Q_WIDTH = N_Q_HEADS * HEAD_DIM
KV_WIDTH = N_KV_HEADS * HEAD_DIM
D_FF = 2816
CONV_WIDTH = 3
N_MOD = 6
EPS = 1e-6
NEG_INF = -1e30

LANES = 128
SUBLANES_F32 = 8
SUBLANES_BF16 = 16
VMEM_BYTES_V7X = 64 << 20

KV_POS_WIDTH = 2 * N_KV_HEADS * LANES
PAIR_WIDTH = 2 * HEAD_DIM
A_WIDTH = Q_WIDTH + 2 * KV_WIDTH + POOL_WIDTH
IN_WIDTH = A_WIDTH + 2 * D_MODEL
POOL_HALO = SUBLANES_F32
CONV_HALO = SUBLANES_F32
UP_AHEAD = 2
FF_CHUNK = 256
N_FF_CHUNKS = D_FF // FF_CHUNK
MOD_ROWS = 16
TM_A, TQ_B, TM_C = 512, 512, 512
SUB_B = SUB_C = 256

F32 = jnp.float32
BF16 = jnp.bfloat16


def _sigmoid(v):
    return 1.0 / (1.0 + jnp.exp(-v))


def _modulate(x, g, shift, scale):
    ms = jnp.mean(x * x, axis=-1, keepdims=True)
    y = x * lax.rsqrt(ms + EPS)
    return (y * g) * (1.0 + scale) + shift


def _dot(a, b):
    return jnp.dot(a, b, preferred_element_type=F32)


def _dot_t(a, b):
    return lax.dot_general(a, b, (((1,), (1,)), ((), ())), preferred_element_type=F32)


def _vmem_limit(pipelined_bytes, resident_bytes, scratch_bytes, temp_bytes):
    need = 2 * pipelined_bytes + 2 * resident_bytes + scratch_bytes + temp_bytes
    assert need < VMEM_BYTES_V7X, need
    return int(need)


def _layer_resident(layer, shape):
    return pl.BlockSpec((None,) + tuple(shape), lambda *_: (layer,) + (0,) * len(shape))


def _adaln_kernel(c_ref, w_ref, b_ref, o_ref):
    c = c_ref[...]
    s = c * _sigmoid(c)
    o_ref[0] = jnp.dot(s, w_ref[0], preferred_element_type=F32,
                       precision=lax.Precision.HIGHEST) + b_ref[0]


def _adaln(cond, w_mod, b_mod):
    n_out = N_MOD * D_MODEL
    nc = n_out // 4
    return pl.pallas_call(
        _adaln_kernel,
        grid=(DEPTH, n_out // nc),
        in_specs=[
            pl.BlockSpec((MOD_ROWS, D_MODEL), lambda l, j: (0, 0)),
            pl.BlockSpec((1, D_MODEL, nc), lambda l, j: (l, 0, j)),
            pl.BlockSpec((1, 1, nc), lambda l, j: (l, 0, j)),
        ],
        out_specs=pl.BlockSpec((1, MOD_ROWS, nc), lambda l, j: (l, 0, j)),
        out_shape=jax.ShapeDtypeStruct((DEPTH, MOD_ROWS, n_out), F32),
        compiler_params=pltpu.CompilerParams(
            dimension_semantics=("parallel", "parallel"),
            vmem_limit_bytes=_vmem_limit(D_MODEL * nc * 4 + MOD_ROWS * nc * 8, MOD_ROWS * D_MODEL * 4, 0, 4 << 20)),
        name="adaln",
    )(cond, w_mod, b_mod.reshape(DEPTH, 1, n_out))


def _head_norm_rope(z, gain, rope, out_scale):
    low = lax.broadcasted_iota(jnp.int32, z.shape, 1) < HEAD_DIM
    sq = z * z
    ss_lo = jnp.sum(jnp.where(low, sq, 0.0), axis=-1, keepdims=True)
    ss_hi = jnp.sum(jnp.where(low, 0.0, sq), axis=-1, keepdims=True)
    r_lo = lax.rsqrt(ss_lo * (1.0 / HEAD_DIM) + EPS)
    r_hi = lax.rsqrt(ss_hi * (1.0 / HEAD_DIM) + EPS)
    r = jnp.where(low, r_lo, r_hi)
    t = z * gain
    if rope is not None:
        z_rot, gain_rot, cos, sin = rope
        t = t * cos + (z_rot * gain_rot) * sin
    t = t * r
    if out_scale != 1.0:
        t = t * out_scale
    return t


def _stage_a_kernel(*refs, rope):
    if rope:
        (x_ref, mod_ref, g_ref, w_ref, qg_ref, kg_ref, wr_ref, qgr_ref, kgr_ref, cos_ref, sin_ref,
         q_ref, kz_ref, vz_ref, p_ref) = refs
        cos, sin = cos_ref[...], sin_ref[...]
    else:
        x_ref, mod_ref, g_ref, w_ref, qg_ref, kg_ref, q_ref, kz_ref, vz_ref, p_ref = refs
    d = D_MODEL
    m = mod_ref[0]
    h = _modulate(x_ref[0], g_ref[...], m[:, 0:d], m[:, d:2 * d]).astype(BF16)

    qg, kg = qg_ref[...], kg_ref[...]
    zq = _dot(h, w_ref[:, 0:Q_WIDTH])
    zq_rot = _dot(h, wr_ref[:, 0:Q_WIDTH]) if rope else None
    for c in range(Q_WIDTH // LANES):
        sl = slice(c * LANES, (c + 1) * LANES)
        rp = (zq_rot[:, sl], qgr_ref[...], cos, sin) if rope else None
        q_ref[0, :, sl] = _head_norm_rope(zq[:, sl], qg, rp, HEAD_DIM ** -0.5).astype(BF16)

    k0 = Q_WIDTH
    p0 = k0 + 2 * KV_WIDTH
    zkv = _dot(h, w_ref[:, k0:p0])
    rp = (_dot(h, wr_ref[:, Q_WIDTH:Q_WIDTH + KV_WIDTH]), kgr_ref[...], cos, sin) if rope else None
    kk = _head_norm_rope(zkv[:, 0:KV_WIDTH], kg, rp, 1.0)
    vv = zkv[:, KV_WIDTH:2 * KV_WIDTH]
    low = lax.broadcasted_iota(jnp.int32, kk.shape, 1) < HEAD_DIM
    for src, dst in ((kk, kz_ref), (vv, vz_ref)):
        swapped = pltpu.roll(src, HEAD_DIM, 1)
        dst[0, :, 0 * LANES:1 * LANES] = jnp.where(low, src, 0.0).astype(BF16)
        dst[0, :, 1 * LANES:2 * LANES] = jnp.where(low, 0.0, swapped).astype(BF16)
        dst[0, :, 2 * LANES:3 * LANES] = jnp.where(low, swapped, 0.0).astype(BF16)
        dst[0, :, 3 * LANES:4 * LANES] = jnp.where(low, 0.0, src).astype(BF16)

    p_ref[0] = _dot(h, w_ref[:, p0:p0 + POOL_WIDTH])


def _stage_a(x, mods, mod_row, layer, norm_g, w_in, q_gain, k_gain, rope_args, *, tm):
    bsz, n, d = x.shape
    wa = A_WIDTH
    rope = rope_args is not None
    in_specs = [
        pl.BlockSpec((1, tm, d), lambda i, b: (b, i, 0)),
        pl.BlockSpec((1, 1, N_MOD * d), lambda i, b: (mod_row(b), 0, 0)),
        _layer_resident(layer, (1, d)),
        _layer_resident(layer, (d, wa)),
        _layer_resident(layer, (1, LANES)),
        _layer_resident(layer, (1, LANES)),
    ]
    args = [x, mods, norm_g, w_in, q_gain, k_gain]
    if rope:
        w_rot, q_gain_rot, k_gain_rot, cos, sin = rope_args
        in_specs += [_layer_resident(layer, (d, Q_WIDTH + KV_WIDTH)), _layer_resident(layer, (1, LANES)),
                     _layer_resident(layer, (1, LANES))]
        in_specs += [pl.BlockSpec((tm, LANES), lambda i, b: (i, 0))] * 2
        args += [w_rot, q_gain_rot, k_gain_rot, cos, sin]
    out_shape = [
        jax.ShapeDtypeStruct((bsz, n, Q_WIDTH), BF16),
        jax.ShapeDtypeStruct((bsz, n, KV_POS_WIDTH), BF16),
        jax.ShapeDtypeStruct((bsz, n, KV_POS_WIDTH), BF16),
        jax.ShapeDtypeStruct((bsz, n, POOL_WIDTH), F32),
    ]
    out_specs = [pl.BlockSpec((1, tm, s.shape[2]), lambda i, b: (b, i, 0)) for s in out_shape]
    pipelined = tm * d * 4 + tm * (Q_WIDTH + 2 * KV_POS_WIDTH) * 2 + tm * POOL_WIDTH * 4 + 2 * tm * LANES * 4
    return pl.pallas_call(
        functools.partial(_stage_a_kernel, rope=rope),
        grid=(n // tm, bsz),
        in_specs=in_specs,
        out_specs=out_specs,
        out_shape=out_shape,
        compiler_params=pltpu.CompilerParams(
            dimension_semantics=("parallel", "parallel"),
            vmem_limit_bytes=_vmem_limit(pipelined, d * (wa + Q_WIDTH + KV_WIDTH) * 2, 0, 8 * tm * d * 4)),
        name="stage_a_rope" if rope else "stage_a",
    )(*args)


def _pool_diff(pbuf, tq, seq_len, tile_start):
    pos = tile_start + lax.broadcasted_iota(jnp.int32, (tq, POOL_GROUP_DIM), 0)
    outs = []
    for gi, w in enumerate(POOL_SIZES):
        cols = slice(gi * POOL_GROUP_DIM, (gi + 1) * POOL_GROUP_DIM)
        lo_off = -(w // 2)
        hi_off = w - w // 2
        acc = pbuf[pl.ds(POOL_HALO + lo_off, tq), cols]
        for j in range(lo_off + 1, hi_off):
            acc = acc + pbuf[pl.ds(POOL_HALO + j, tq), cols]
        cnt = (jnp.minimum(pos + hi_off, seq_len) - jnp.maximum(pos + lo_off, 0)).astype(F32)
        outs.append(acc / cnt - pbuf[pl.ds(POOL_HALO, tq), cols])
    return outs


def _scores(lhs, pieces):
    out = []
    for idx in (0, 1):
        s = []
        for pc in pieces:
            sc = _dot_t(lhs, pc[idx])
            if pc[4] is not None:
                sc = jnp.where(pc[4], sc, NEG_INF)
            s.append(sc)
        out.append(s)
    return out


def _softmax_pv(scores, pieces, sink_a, sink_b):
    def side(s, sink):
        m = sink
        for sc in s:
            m = jnp.maximum(m, jnp.max(sc, axis=-1, keepdims=True))
        e = [jnp.exp(sc - m) for sc in s]
        den = jnp.sum(e[0], axis=-1, keepdims=True)
        for ee in e[1:]:
            den = den + jnp.sum(ee, axis=-1, keepdims=True)
        den = den + jnp.exp(sink - m)
        return e, den

    e_a, den_a = side(scores[0], sink_a)
    e_b, den_b = side(scores[1], sink_b)
    o = None
    for pc, ea, eb in zip(pieces, e_a, e_b):
        t = _dot(ea.astype(BF16), pc[2]) + _dot(eb.astype(BF16), pc[3])
        o = t if o is None else o + t
    lane = lax.broadcasted_iota(jnp.int32, o.shape, 1)
    return o / jnp.where(lane < HEAD_DIM, den_a, den_b)


def _stage_b_kernel(*refs, tq, seq_len, latent, layer):
    if latent:
        (x_ref, mod_ref, g_ref, q_ref, kp_ref, kc_ref, kn_ref, vp_ref, vc_ref, vn_ref, kx_ref, vx_ref,
         pp_ref, pc_ref, pn_ref, sink_ref, wg_ref, wpool_ref, psc_ref, wba_ref, wbp_ref, wo_ref,
         o_ref, kwin, vwin, pbuf, attn) = refs
    else:
        (x_ref, mod_ref, g_ref, q_ref, kc_ref, vc_ref,
         pp_ref, pc_ref, pn_ref, sink_ref, wg_ref, wpool_ref, psc_ref, wba_ref, wbp_ref, wo_ref,
         o_ref, pbuf, attn) = refs
    d = D_MODEL
    i = pl.program_id(1)
    n_tiles = seq_len // tq
    n_qb = tq // BLOCK
    n_blocks = seq_len // BLOCK

    x = x_ref[0]
    m = mod_ref[0]
    h = _modulate(x, g_ref[...], m[:, 0:d], m[:, d:2 * d]).astype(BF16)

    if latent:
        kwin[0:BLOCK] = kp_ref[0]
        kwin[BLOCK:BLOCK + tq] = kc_ref[0]
        kwin[BLOCK + tq:2 * BLOCK + tq] = kn_ref[0]
        vwin[0:BLOCK] = vp_ref[0]
        vwin[BLOCK:BLOCK + tq] = vc_ref[0]
        vwin[BLOCK + tq:2 * BLOCK + tq] = vn_ref[0]
        row = lax.broadcasted_iota(jnp.int32, (2 * BLOCK, 3 * BLOCK), 0) & (BLOCK - 1)
        col = lax.broadcasted_iota(jnp.int32, (2 * BLOCK, 3 * BLOCK), 1)
        band = (col >= row) & (col <= row + 2 * WINDOW)
    srow = lax.broadcasted_iota(jnp.int32, (2 * BLOCK, 1), 0)

    def unit_pieces(ib, hk):
        r0 = ib * BLOCK
        q0 = hk * Q_GROUP * HEAD_DIM
        lhs = jnp.concatenate([q_ref[0, r0:r0 + BLOCK, q0:q0 + PAIR_WIDTH],
                               q_ref[0, r0:r0 + BLOCK, q0 + PAIR_WIDTH:q0 + 2 * PAIR_WIDTH]], axis=0)
        ca = slice((2 * hk) * LANES, (2 * hk + 1) * LANES)
        cb = slice((2 * hk + 1) * LANES, (2 * hk + 2) * LANES)
        if latent:
            g = i * n_qb + ib
            lo = jnp.where(g > 0, 0, BLOCK)
            hi = jnp.where(g < n_blocks - 1, 3 * BLOCK, 2 * BLOCK)
            mask = band & (col >= lo) & (col < hi)
            rows = slice(r0, r0 + 3 * BLOCK)
            pieces = [(kwin[rows, ca], kwin[rows, cb], vwin[rows, ca], vwin[rows, cb], mask),
                      (kx_ref[0, :, ca], kx_ref[0, :, cb], vx_ref[0, :, ca], vx_ref[0, :, cb], None)]
        else:
            pieces = [(kc_ref[0, :, ca], kc_ref[0, :, cb], vc_ref[0, :, ca], vc_ref[0, :, cb], None)]
        return lhs, pieces

    def unit_scores(ib, hk):
        lhs, pieces = unit_pieces(ib, hk)
        return _scores(lhs, pieces), pieces

    def unit_finish(ib, hk, scores, pieces):
        r0 = ib * BLOCK
        q0 = hk * Q_GROUP * HEAD_DIM
        h0 = hk * Q_GROUP
        sink_a = jnp.where(srow < BLOCK, sink_ref[layer, h0], sink_ref[layer, h0 + 2])
        sink_b = jnp.where(srow < BLOCK, sink_ref[layer, h0 + 1], sink_ref[layer, h0 + 3])
        o = _softmax_pv(scores, pieces, sink_a, sink_b).astype(BF16)
        attn[r0:r0 + BLOCK, q0:q0 + PAIR_WIDTH] = o[0:BLOCK]
        attn[r0:r0 + BLOCK, q0 + PAIR_WIDTH:q0 + 2 * PAIR_WIDTH] = o[BLOCK:2 * BLOCK]

    pbuf[0:POOL_HALO] = jnp.where(i > 0, pp_ref[0], 0.0)
    pbuf[POOL_HALO:POOL_HALO + tq] = pc_ref[0]
    pbuf[POOL_HALO + tq:2 * POOL_HALO + tq] = jnp.where(i < n_tiles - 1, pn_ref[0], 0.0)
    pool_d = [dg.astype(BF16) for dg in _pool_diff(pbuf, tq, seq_len, i * tq)]

    n_sub = tq // SUB_B
    qb_per_sub = SUB_B // BLOCK

    def slab(j):
        return slice(j * SUB_B, (j + 1) * SUB_B)

    def gate(j, col0):
        return _sigmoid(_dot(h[slab(j)], wg_ref[:, col0:col0 + d]))

    def pool_branch(j):
        pooled = []
        for gi, dg in enumerate(pool_d):
            cols = slice(gi * POOL_GROUP_DIM, (gi + 1) * POOL_GROUP_DIM)
            pooled.append(_dot(dg[slab(j)], wpool_ref[gi]) * psc_ref[:, cols])
        return _dot(jnp.concatenate(pooled, axis=-1).astype(BF16), wbp_ref[...])

    def merge(j, gate_attn, gate_pool, z_pool):
        r = slab(j)
        y = gate_attn * _dot(attn[r, :], wba_ref[...]) + gate_pool * z_pool
        o_ref[0, r, :] = x[r] + m[:, 2 * d:3 * d] * _dot(y.astype(BF16), wo_ref[...])

    units = [(ib, hk) for ib in range(n_qb) for hk in range(N_KV_HEADS)]
    units_per_sub = qb_per_sub * N_KV_HEADS
    dense = []
    for j in range(n_sub):
        dense += [functools.partial(gate, j, A_WIDTH), functools.partial(gate, j, A_WIDTH + d),
                  functools.partial(pool_branch, j)]
    assert units_per_sub >= 3
    dense_out = []
    pending = unit_scores(*units[0])
    for k, u in enumerate(units):
        nxt = unit_scores(*units[k + 1]) if k + 1 < len(units) else None
        if k < len(dense):
            dense_out.append(dense[k]())
        unit_finish(*u, *pending)
        pending = nxt
        if (k + 1) % units_per_sub == 0:
            j = k // units_per_sub
            assert len(dense_out) >= 3 * (j + 1)
            merge(j, *dense_out[3 * j:3 * j + 3])


def _stage_b(x, mods, mod_row, layer, norm_g, q, kz, vz, ctx_kv, p, sink, w_in, w_pool, pool_scale,
             w_br_attn, w_br_pool, w_out, *, tq):
    bsz, n, d = x.shape
    latent = ctx_kv is not None
    n_qb = tq // BLOCK
    nb = n // BLOCK
    tile = lambda b, i: (b, i, 0)
    in_specs = [
        pl.BlockSpec((1, tq, d), tile),
        pl.BlockSpec((1, 1, N_MOD * d), lambda b, i: (mod_row(b), 0, 0)),
        _layer_resident(layer, (1, d)),
        pl.BlockSpec((1, tq, Q_WIDTH), tile),
    ]
    args = [x, mods, norm_g, q]
    kv_cur = pl.BlockSpec((1, tq, KV_POS_WIDTH), tile)
    if latent:
        kv_prev = pl.BlockSpec((1, BLOCK, KV_POS_WIDTH), lambda b, i: (b, jnp.maximum(i * n_qb - 1, 0), 0))
        kv_next = pl.BlockSpec((1, BLOCK, KV_POS_WIDTH), lambda b, i: (b, jnp.minimum((i + 1) * n_qb, nb - 1), 0))
        n_ctx = ctx_kv[0].shape[1]
        kv_ctx = pl.BlockSpec((1, n_ctx, KV_POS_WIDTH), lambda b, i: (b, 0, 0))
        in_specs += [kv_prev, kv_cur, kv_next, kv_prev, kv_cur, kv_next, kv_ctx, kv_ctx]
        args += [kz, kz, kz, vz, vz, vz, ctx_kv[0], ctx_kv[1]]
    else:
        in_specs += [kv_cur, kv_cur]
        args += [kz, vz]
    n_halo = n // POOL_HALO
    per_tile = tq // POOL_HALO
    in_specs += [
        pl.BlockSpec((1, POOL_HALO, POOL_WIDTH), lambda b, i: (b, jnp.maximum(i * per_tile - 1, 0), 0)),
        pl.BlockSpec((1, tq, POOL_WIDTH), tile),
        pl.BlockSpec((1, POOL_HALO, POOL_WIDTH), lambda b, i: (b, jnp.minimum((i + 1) * per_tile, n_halo - 1), 0)),
        pl.BlockSpec(memory_space=pltpu.SMEM),
        _layer_resident(layer, (d, IN_WIDTH)),
        _layer_resident(layer, (N_POOL_GROUPS, POOL_GROUP_DIM, POOL_GROUP_DIM)),
        _layer_resident(layer, (1, POOL_WIDTH)),
        _layer_resident(layer, (Q_WIDTH, d)),
        _layer_resident(layer, (POOL_WIDTH, d)),
        _layer_resident(layer, (d, d)),
    ]
    args += [p, p, p, sink, w_in, w_pool, pool_scale, w_br_attn, w_br_pool, w_out]
    scratch = []
    scratch_bytes = 0
    if latent:
        scratch += [pltpu.VMEM((tq + 2 * BLOCK, KV_POS_WIDTH), BF16)] * 2
        scratch_bytes += 2 * (tq + 2 * BLOCK) * KV_POS_WIDTH * 2
    scratch += [pltpu.VMEM((tq + 2 * POOL_HALO, POOL_WIDTH), F32), pltpu.VMEM((tq, Q_WIDTH), BF16)]
    scratch_bytes += (tq + 2 * POOL_HALO) * POOL_WIDTH * 4 + tq * Q_WIDTH * 2
    pipelined = (2 * tq * d * 4 + tq * Q_WIDTH * 2 + 2 * (tq + 3 * BLOCK) * KV_POS_WIDTH * 2
                 + (tq + 2 * POOL_HALO) * POOL_WIDTH * 4)
    resident = (IN_WIDTH * d + Q_WIDTH * d + POOL_WIDTH * d + d * d) * 2
    return pl.pallas_call(
        functools.partial(_stage_b_kernel, tq=tq, seq_len=n, latent=latent, layer=layer),
        grid=(bsz, n // tq),
        in_specs=in_specs,
        out_specs=pl.BlockSpec((1, tq, d), tile),
        out_shape=jax.ShapeDtypeStruct((bsz, n, d), F32),
        scratch_shapes=scratch,
        compiler_params=pltpu.CompilerParams(
            dimension_semantics=("parallel", "parallel"),
            vmem_limit_bytes=_vmem_limit(pipelined, resident, scratch_bytes, 12 * SUB_B * d * 4 + 4 * tq * d * 4)),
        name="stage_b_latent" if latent else "stage_b_ctx",
    )(*args)


def _stage_c_kernel(x_ref, xp_ref, xn_ref, mod_ref, g_ref, wu_ref, cw_ref, cb_ref, wd_ref, o_ref,
                    hbuf, *, tm, seq_len):
    d = D_MODEL
    i = pl.program_id(1)
    n_tiles = seq_len // tm
    n_sub = tm // SUB_C
    m = mod_ref[0]
    g = g_ref[...]
    shift, scale = m[:, 3 * d:4 * d], m[:, 4 * d:5 * d]
    x = x_ref[0]
    hp = jnp.where(i > 0, _modulate(xp_ref[0], g, shift, scale), 0.0)
    hn = jnp.where(i < n_tiles - 1, _modulate(xn_ref[0], g, shift, scale), 0.0)
    h = _modulate(x, g, shift, scale)
    rows = SUB_C + 2 * CONV_HALO
    assert SUB_C % SUBLANES_BF16 == 0 and (2 * CONV_HALO) % SUBLANES_BF16 == 0
    for j in range(n_sub):
        r0 = j * SUB_C
        nxt = hn if j == n_sub - 1 else h[r0 + SUB_C:r0 + SUB_C + CONV_HALO]
        prv = hp if j == 0 else h[r0 - CONV_HALO:r0]
        hbuf[j, 0:SUB_C] = h[r0:r0 + SUB_C].astype(BF16)
        hbuf[j, SUB_C:rows] = jnp.concatenate([nxt, prv], axis=0).astype(BF16)

    def conv(u, cols):
        uc = cb_ref[:, cols] + pltpu.roll(u, 1, 0)[0:SUB_C] * cw_ref[0:1, cols]
        uc = uc + u[0:SUB_C] * cw_ref[1:2, cols]
        return uc + pltpu.roll(u, rows - 1, 0)[0:SUB_C] * cw_ref[2:3, cols]

    def conv_act(u, c):
        a = conv(u[0], slice(c * FF_CHUNK, (c + 1) * FF_CHUNK))
        b = conv(u[1], slice(D_FF + c * FF_CHUNK, D_FF + (c + 1) * FF_CHUNK))
        return ((a * _sigmoid(a)) * b).astype(BF16)

    def up(j, c):
        return [_dot(hbuf[j], wu_ref[:, off + c * FF_CHUNK:off + (c + 1) * FF_CHUNK]) for off in (0, D_FF)]

    acc = [None] * n_sub
    act_prev = [None] * n_sub
    u_queue = [[up(j, c) for c in range(UP_AHEAD)] for j in range(n_sub)]
    for c in range(N_FF_CHUNKS + 1):
        for j in range(n_sub):
            if c + UP_AHEAD < N_FF_CHUNKS:
                u_queue[j].append(up(j, c + UP_AHEAD))
            act = conv_act(u_queue[j].pop(0), c) if c < N_FF_CHUNKS else None
            if c >= 1:
                krows = slice((c - 1) * FF_CHUNK, c * FF_CHUNK)
                t = [_dot(act_prev[j], wd_ref[krows, k * (d // 2):(k + 1) * (d // 2)]) for k in range(2)]
                acc[j] = t if acc[j] is None else [a + b for a, b in zip(acc[j], t)]
            act_prev[j] = act
    for j in range(n_sub):
        r = slice(j * SUB_C, (j + 1) * SUB_C)
        o_ref[0, r, :] = x[r] + m[:, 5 * d:6 * d] * jnp.concatenate(acc[j], axis=-1)


def _stage_c(x, mods, mod_row, layer, norm_g, w_up, conv_w, conv_b, w_down, *, tm):
    bsz, n, d = x.shape
    tile = lambda b, i: (b, i, 0)
    per_tile = tm // CONV_HALO
    n_halo = n // CONV_HALO
    in_specs = [
        pl.BlockSpec((1, tm, d), tile),
        pl.BlockSpec((1, CONV_HALO, d), lambda b, i: (b, jnp.maximum(i * per_tile - 1, 0), 0)),
        pl.BlockSpec((1, CONV_HALO, d), lambda b, i: (b, jnp.minimum((i + 1) * per_tile, n_halo - 1), 0)),
        pl.BlockSpec((1, 1, N_MOD * d), lambda b, i: (mod_row(b), 0, 0)),
        _layer_resident(layer, (1, d)),
        _layer_resident(layer, (d, 2 * D_FF)),
        _layer_resident(layer, (CONV_WIDTH, 2 * D_FF)),
        _layer_resident(layer, (1, 2 * D_FF)),
        _layer_resident(layer, (D_FF, d)),
    ]
    assert tm % SUB_C == 0
    hbuf_shape = (tm // SUB_C, SUB_C + 2 * CONV_HALO, d)
    scratch_bytes = hbuf_shape[0] * hbuf_shape[1] * d * 2
    return pl.pallas_call(
        functools.partial(_stage_c_kernel, tm=tm, seq_len=n),
        grid=(bsz, n // tm),
        in_specs=in_specs,
        out_specs=pl.BlockSpec((1, tm, d), tile),
        out_shape=jax.ShapeDtypeStruct((bsz, n, d), F32),
        scratch_shapes=[pltpu.VMEM(hbuf_shape, BF16)],
        compiler_params=pltpu.CompilerParams(
            dimension_semantics=("parallel", "parallel"),
            vmem_limit_bytes=_vmem_limit(2 * tm * d * 4 + 2 * CONV_HALO * d * 4, 3 * d * D_FF * 2,
                                         scratch_bytes, 8 * tm * d * 4)),
        name="stage_c",
    )(x, x, x, mods, norm_g, w_up, conv_w, conv_b, w_down)


def _rope_tables(n_tok):
    rows = n_tok // GRID_W
    row = jnp.repeat(jnp.arange(rows, dtype=jnp.int32), GRID_W).astype(F32)
    col = jnp.tile(jnp.arange(GRID_W, dtype=jnp.int32), rows).astype(F32)
    n_freq = HEAD_DIM // 4
    inv = ROPE_THETA ** (-jnp.arange(n_freq, dtype=F32) / n_freq)
    ar, ac = row[:, None] * inv, col[:, None] * inv
    cos = jnp.concatenate([jnp.cos(ar), jnp.cos(ar), jnp.cos(ac), jnp.cos(ac)], axis=-1)
    sin = jnp.concatenate([-jnp.sin(ar), jnp.sin(ar), -jnp.sin(ac), jnp.sin(ac)], axis=-1)
    return jnp.tile(cos, (1, LANES // HEAD_DIM)), jnp.tile(sin, (1, LANES // HEAD_DIM))


def _swap_rope_halves(a):
    quarter = HEAD_DIM // 4
    upper = (jnp.arange(a.shape[-1]) & quarter) != 0
    return jnp.where(upper, jnp.roll(a, quarter, axis=-1), jnp.roll(a, -quarter, axis=-1))


def kernel(x, c, ctx, c_ctx, w_mod, b_mod, norm1_g, norm2_g, w_in, q_gain, k_gain, sink, w_pool, pool_scale,
           w_br_attn, w_br_pool, w_out, w_up, conv_w, conv_b, w_down):
    bsz, n_tok, d = x.shape
    n_ctx = ctx.shape[1]
    assert bsz + 1 <= MOD_ROWS
    cond = jnp.zeros((MOD_ROWS, d), F32).at[:bsz].set(c).at[bsz].set(c_ctx)
    mods = _adaln(cond, w_mod, b_mod).reshape(DEPTH * MOD_ROWS, 1, N_MOD * d)

    w_in_b = w_in.astype(BF16)
    g1 = norm1_g.reshape(DEPTH, 1, d)
    g2 = norm2_g.reshape(DEPTH, 1, d)
    qg = jnp.tile(q_gain, (1, LANES // HEAD_DIM)).reshape(DEPTH, 1, LANES)
    kg = jnp.tile(k_gain, (1, LANES // HEAD_DIM)).reshape(DEPTH, 1, LANES)
    rope_args = (_swap_rope_halves(w_in_b[:, :, :Q_WIDTH + KV_WIDTH]),
                 _swap_rope_halves(qg), _swap_rope_halves(kg)) + _rope_tables(n_tok)
    b_weights = (sink, w_in_b, w_pool.astype(BF16), pool_scale.reshape(DEPTH, 1, POOL_WIDTH),
                 w_br_attn.astype(BF16), w_br_pool.astype(BF16), w_out.astype(BF16))
    c_weights = (g2, w_up.astype(BF16), conv_w, conv_b.reshape(DEPTH, 1, 2 * D_FF), w_down.astype(BF16))

    xc = ctx
    for l in range(DEPTH):
        last = l == DEPTH - 1
        lat_row = lambda b, l=l: l * MOD_ROWS + b
        ctx_row = lambda b, l=l: l * MOD_ROWS + bsz
        qc, kzc, vzc, pc = _stage_a(xc, mods, ctx_row, l, g1, w_in_b, qg, kg, None, tm=n_ctx)
        q, kz, vz, p = _stage_a(x, mods, lat_row, l, g1, w_in_b, qg, kg, rope_args, tm=TM_A)
        x = _stage_b(x, mods, lat_row, l, g1, q, kz, vz, (kzc, vzc), p, *b_weights, tq=TQ_B)
        x = _stage_c(x, mods, lat_row, l, *c_weights, tm=TM_C)
        if not last:
            xc = _stage_b(xc, mods, ctx_row, l, g1, qc, kzc, vzc, None, pc, *b_weights, tq=n_ctx)
            xc = _stage_c(xc, mods, ctx_row, l, *c_weights, tm=n_ctx)
    return x
```

```python
import functools

import jax
import jax.numpy as jnp
from jax import lax
from jax.experimental import pallas as pl
from jax.experimental.pallas import tpu as pltpu

D_MODEL = 1024
DEPTH = 4
GRID_W = 64
N_Q_HEADS = 8
N_KV_HEADS = 2
HEAD_DIM = 64
Q_GROUP = N_Q_HEADS // N_KV_HEADS
WINDOW = 128
BLOCK = 128
ROPE_THETA = 10000.0
POOL_SIZES = (2, 4, 8, 16)
N_POOL_GROUPS = len(POOL_SIZES)
POOL_GROUP_DIM = D_MODEL // 8
POOL_WIDTH = N_POOL_GROUPS * POOL_GROUP_DIM
Q_WIDTH = N_Q_HEADS * HEAD_DIM
KV_WIDTH = N_KV_HEADS * HEAD_DIM
D_FF = 2816
CONV_WIDTH = 3
N_MOD = 6
EPS = 1e-6
NEG_INF = -1e30

LANES = 128
SUBLANES_F32 = 8
SUBLANES_BF16 = 16
VMEM_BYTES_V7X = 64 << 20

KV_POS_WIDTH = 2 * N_KV_HEADS * LANES
PAIR_WIDTH = 2 * HEAD_DIM
A_WIDTH = Q_WIDTH + 2 * KV_WIDTH + POOL_WIDTH
IN_WIDTH = A_WIDTH + 2 * D_MODEL
POOL_HALO = SUBLANES_F32
DOUBLING_MIN_WIDTH = 8
CONV_HALO = SUBLANES_F32
UP_AHEAD = 3
FF_CHUNK = 256
N_FF_CHUNKS = D_FF // FF_CHUNK
MOD_ROWS = 16
TM_A, TQ_B, TM_C = 1024, 512, 512
SUB_A = SUB_B = SUB_C = 256

F32 = jnp.float32
BF16 = jnp.bfloat16


def _sigmoid(v):
    return 1.0 / (1.0 + jnp.exp(-v))


def _modulate(x, g, shift, scale):
    ms = jnp.mean(x * x, axis=-1, keepdims=True)
    y = x * lax.rsqrt(ms + EPS)
    return (y * g) * (1.0 + scale) + shift


def _dot(a, b):
    return jnp.dot(a, b, preferred_element_type=F32)


def _dot_t(a, b):
    return lax.dot_general(a, b, (((1,), (1,)), ((), ())), preferred_element_type=F32)


def _vmem_limit(pipelined_bytes, resident_bytes, scratch_bytes, temp_bytes):
    need = 2 * pipelined_bytes + 2 * resident_bytes + scratch_bytes + temp_bytes
    assert need < VMEM_BYTES_V7X, need
    return int(need)


def _layer_resident(layer, shape):
    return pl.BlockSpec((None,) + tuple(shape), lambda *_: (layer,) + (0,) * len(shape))


def _adaln_kernel(c_ref, w_ref, b_ref, o_ref):
    c = c_ref[...]
    s = c * _sigmoid(c)
    o_ref[0] = jnp.dot(s, w_ref[0], preferred_element_type=F32,
                       precision=lax.Precision.HIGHEST) + b_ref[0]


def _adaln(cond, w_mod, b_mod):
    n_out = N_MOD * D_MODEL
    nc = n_out // 4
    return pl.pallas_call(
        _adaln_kernel,
        grid=(DEPTH, n_out // nc),
        in_specs=[
            pl.BlockSpec((MOD_ROWS, D_MODEL), lambda l, j: (0, 0)),
            pl.BlockSpec((1, D_MODEL, nc), lambda l, j: (l, 0, j)),
            pl.BlockSpec((1, 1, nc), lambda l, j: (l, 0, j)),
        ],
        out_specs=pl.BlockSpec((1, MOD_ROWS, nc), lambda l, j: (l, 0, j)),
        out_shape=jax.ShapeDtypeStruct((DEPTH, MOD_ROWS, n_out), F32),
        compiler_params=pltpu.CompilerParams(
            dimension_semantics=("parallel", "parallel"),
            vmem_limit_bytes=_vmem_limit(D_MODEL * nc * 4 + MOD_ROWS * nc * 8, MOD_ROWS * D_MODEL * 4, 0, 4 << 20)),
        name="adaln",
    )(cond, w_mod, b_mod.reshape(DEPTH, 1, n_out))


def _head_norm_rope(z, gain, rope, out_scale):
    low = lax.broadcasted_iota(jnp.int32, z.shape, 1) < HEAD_DIM
    sq = z * z
    ss_lo = jnp.sum(jnp.where(low, sq, 0.0), axis=-1, keepdims=True)
    ss_hi = jnp.sum(jnp.where(low, 0.0, sq), axis=-1, keepdims=True)
    r_lo = lax.rsqrt(ss_lo * (1.0 / HEAD_DIM) + EPS)
    r_hi = lax.rsqrt(ss_hi * (1.0 / HEAD_DIM) + EPS)
    r = jnp.where(low, r_lo, r_hi)
    t = z * gain
    if rope is not None:
        z_rot, gain_rot, cos, sin = rope
        t = t * cos + (z_rot * gain_rot) * sin
    t = t * r
    if out_scale != 1.0:
        t = t * out_scale
    return t


def _stage_a_kernel(*refs, rope):
    if rope:
        (x_ref, mod_ref, g_ref, w_ref, qg_ref, kg_ref, wr_ref, qgr_ref, kgr_ref, cos_ref, sin_ref,
         q_ref, kz_ref, vz_ref, p_ref) = refs
    else:
        x_ref, mod_ref, g_ref, w_ref, qg_ref, kg_ref, q_ref, kz_ref, vz_ref, p_ref = refs
    d = D_MODEL
    m = mod_ref[0]
    qg, kg = qg_ref[...], kg_ref[...]
    k0 = Q_WIDTH
    p0 = k0 + 2 * KV_WIDTH
    tm = x_ref.shape[1]
    sub = min(tm, SUB_A)

    def project(r):
        h = _modulate(x_ref[0, r, :], g_ref[...], m[:, 0:d], m[:, d:2 * d]).astype(BF16)
        z = {"q": _dot(h, w_ref[:, 0:Q_WIDTH]), "kv": _dot(h, w_ref[:, k0:p0])}
        if rope:
            z["q_rot"] = _dot(h, wr_ref[:, 0:Q_WIDTH])
            z["k_rot"] = _dot(h, wr_ref[:, Q_WIDTH:Q_WIDTH + KV_WIDTH])
        p_ref[0, r, :] = _dot(h, w_ref[:, p0:p0 + POOL_WIDTH])
        return z

    def finish(r, z):
        for c in range(Q_WIDTH // LANES):
            sl = slice(c * LANES, (c + 1) * LANES)
            rp = (z["q_rot"][:, sl], qgr_ref[...], cos_ref[r, :], sin_ref[r, :]) if rope else None
            q_ref[0, r, sl] = _head_norm_rope(z["q"][:, sl], qg, rp, HEAD_DIM ** -0.5).astype(BF16)
        rp = (z["k_rot"], kgr_ref[...], cos_ref[r, :], sin_ref[r, :]) if rope else None
        kk = _head_norm_rope(z["kv"][:, 0:KV_WIDTH], kg, rp, 1.0)
        vv = z["kv"][:, KV_WIDTH:2 * KV_WIDTH]
        low = lax.broadcasted_iota(jnp.int32, kk.shape, 1) < HEAD_DIM
        for src, dst in ((kk, kz_ref), (vv, vz_ref)):
            swapped = pltpu.roll(src, HEAD_DIM, 1)
            dst[0, r, 0 * LANES:1 * LANES] = jnp.where(low, src, 0.0).astype(BF16)
            dst[0, r, 1 * LANES:2 * LANES] = jnp.where(low, 0.0, swapped).astype(BF16)
            dst[0, r, 2 * LANES:3 * LANES] = jnp.where(low, swapped, 0.0).astype(BF16)
            dst[0, r, 3 * LANES:4 * LANES] = jnp.where(low, 0.0, src).astype(BF16)

    slabs = [slice(j * sub, (j + 1) * sub) for j in range(tm // sub)]
    pending = None
    for r in slabs:
        z = project(r)
        if pending is not None:
            finish(*pending)
        pending = (r, z)
    finish(*pending)


def _stage_a(x, mods, mod_row, layer, norm_g, w_in, q_gain, k_gain, rope_args, *, tm):
    bsz, n, d = x.shape
    wa = A_WIDTH
    rope = rope_args is not None
    in_specs = [
        pl.BlockSpec((1, tm, d), lambda i, b: (b, i, 0)),
        pl.BlockSpec((1, 1, N_MOD * d), lambda i, b: (mod_row(b), 0, 0)),
        _layer_resident(layer, (1, d)),
        _layer_resident(layer, (d, wa)),
        _layer_resident(layer, (1, LANES)),
        _layer_resident(layer, (1, LANES)),
    ]
    args = [x, mods, norm_g, w_in, q_gain, k_gain]
    if rope:
        w_rot, q_gain_rot, k_gain_rot, cos, sin = rope_args
        in_specs += [_layer_resident(layer, (d, Q_WIDTH + KV_WIDTH)), _layer_resident(layer, (1, LANES)),
                     _layer_resident(layer, (1, LANES))]
        in_specs += [pl.BlockSpec((tm, LANES), lambda i, b: (i, 0))] * 2
        args += [w_rot, q_gain_rot, k_gain_rot, cos, sin]
    out_shape = [
        jax.ShapeDtypeStruct((bsz, n, Q_WIDTH), BF16),
        jax.ShapeDtypeStruct((bsz, n, KV_POS_WIDTH), BF16),
        jax.ShapeDtypeStruct((bsz, n, KV_POS_WIDTH), BF16),
        jax.ShapeDtypeStruct((bsz, n, POOL_WIDTH), F32),
    ]
    out_specs = [pl.BlockSpec((1, tm, s.shape[2]), lambda i, b: (b, i, 0)) for s in out_shape]
    pipelined = tm * d * 4 + tm * (Q_WIDTH + 2 * KV_POS_WIDTH) * 2 + tm * POOL_WIDTH * 4 + 2 * tm * LANES * 4
    return pl.pallas_call(
        functools.partial(_stage_a_kernel, rope=rope),
        grid=(n // tm, bsz),
        in_specs=in_specs,
        out_specs=out_specs,
        out_shape=out_shape,
        compiler_params=pltpu.CompilerParams(
            dimension_semantics=("parallel", "parallel"),
            vmem_limit_bytes=_vmem_limit(pipelined, d * (wa + Q_WIDTH + KV_WIDTH) * 2, 0, 8 * tm * d * 4)),
        name="stage_a_rope" if rope else "stage_a",
    )(*args)


def _pool_diff(pbuf, tq, seq_len, tile_start):
    pos = tile_start + lax.broadcasted_iota(jnp.int32, (tq, POOL_GROUP_DIM), 0)
    rows = tq + 2 * POOL_HALO

    def window_sums(gi, w):
        cols = slice(gi * POOL_GROUP_DIM, (gi + 1) * POOL_GROUP_DIM)
        if w < DOUBLING_MIN_WIDTH:
            acc = pbuf[pl.ds(POOL_HALO - w // 2, tq), cols]
            for j in range(1, w):
                acc = acc + pbuf[pl.ds(POOL_HALO - w // 2 + j, tq), cols]
            return acc
        s = pbuf[...][:, cols]
        span = 1
        while span < w:
            s = s + pltpu.roll(s, rows - span, 0)
            span *= 2
        start = POOL_HALO - w // 2
        return s[start:start + tq]

    outs = []
    for gi, w in enumerate(POOL_SIZES):
        cols = slice(gi * POOL_GROUP_DIM, (gi + 1) * POOL_GROUP_DIM)
        cnt = (jnp.minimum(pos + (w - w // 2), seq_len) - jnp.maximum(pos - w // 2, 0)).astype(F32)
        outs.append(window_sums(gi, w) / cnt - pbuf[pl.ds(POOL_HALO, tq), cols])
    return outs


def _scores(lhs, pieces):
    out = []
    for idx in (0, 1):
        s = []
        for pc in pieces:
            sc = _dot_t(lhs, pc[idx])
            if pc[4] is not None:
                sc = jnp.where(pc[4], sc, NEG_INF)
            s.append(sc)
        out.append(s)
    return out


def _softmax_pv(scores, pieces, sink_a, sink_b):
    def side(s, sink):
        m = sink
        for sc in s:
            m = jnp.maximum(m, jnp.max(sc, axis=-1, keepdims=True))
        e = [jnp.exp(sc - m) for sc in s]
        den = jnp.sum(e[0], axis=-1, keepdims=True)
        for ee in e[1:]:
            den = den + jnp.sum(ee, axis=-1, keepdims=True)
        den = den + jnp.exp(sink - m)
        return e, den

    e_a, den_a = side(scores[0], sink_a)
    e_b, den_b = side(scores[1], sink_b)
    o = None
    for pc, ea, eb in zip(pieces, e_a, e_b):
        t = _dot(ea.astype(BF16), pc[2]) + _dot(eb.astype(BF16), pc[3])
        o = t if o is None else o + t
    lane = lax.broadcasted_iota(jnp.int32, o.shape, 1)
    return o / jnp.where(lane < HEAD_DIM, den_a, den_b)


def _stage_b_kernel(*refs, tq, seq_len, latent, layer):
    if latent:
        (x_ref, mod_ref, g_ref, q_ref, kp_ref, kc_ref, kn_ref, vp_ref, vc_ref, vn_ref, kx_ref, vx_ref,
         pp_ref, pc_ref, pn_ref, sink_ref, wg_ref, wpool_ref, psc_ref, wba_ref, wbp_ref, wo_ref,
         o_ref, kwin, vwin, pbuf, attn) = refs
    else:
        (x_ref, mod_ref, g_ref, q_ref, kc_ref, vc_ref,
         pp_ref, pc_ref, pn_ref, sink_ref, wg_ref, wpool_ref, psc_ref, wba_ref, wbp_ref, wo_ref,
         o_ref, pbuf, attn) = refs
    d = D_MODEL
    i = pl.program_id(1)
    n_tiles = seq_len // tq
    n_qb = tq // BLOCK
    n_blocks = seq_len // BLOCK

    x = x_ref[0]
    m = mod_ref[0]
    h = _modulate(x, g_ref[...], m[:, 0:d], m[:, d:2 * d]).astype(BF16)

    if latent:
        kwin[0:BLOCK] = kp_ref[0]
        kwin[BLOCK:BLOCK + tq] = kc_ref[0]
        kwin[BLOCK + tq:2 * BLOCK + tq] = kn_ref[0]
        vwin[0:BLOCK] = vp_ref[0]
        vwin[BLOCK:BLOCK + tq] = vc_ref[0]
        vwin[BLOCK + tq:2 * BLOCK + tq] = vn_ref[0]
        row = lax.broadcasted_iota(jnp.int32, (2 * BLOCK, 3 * BLOCK), 0) & (BLOCK - 1)
        col = lax.broadcasted_iota(jnp.int32, (2 * BLOCK, 3 * BLOCK), 1)
        band = (col >= row) & (col <= row + 2 * WINDOW)
    srow = lax.broadcasted_iota(jnp.int32, (2 * BLOCK, 1), 0)

    def unit_pieces(ib, hk):
        r0 = ib * BLOCK
        q0 = hk * Q_GROUP * HEAD_DIM
        lhs = jnp.concatenate([q_ref[0, r0:r0 + BLOCK, q0:q0 + PAIR_WIDTH],
                               q_ref[0, r0:r0 + BLOCK, q0 + PAIR_WIDTH:q0 + 2 * PAIR_WIDTH]], axis=0)
        ca = slice((2 * hk) * LANES, (2 * hk + 1) * LANES)
        cb = slice((2 * hk + 1) * LANES, (2 * hk + 2) * LANES)
        if latent:
            g = i * n_qb + ib
            lo = jnp.where(g > 0, 0, BLOCK)
            hi = jnp.where(g < n_blocks - 1, 3 * BLOCK, 2 * BLOCK)
            mask = band & (col >= lo) & (col < hi)
            rows = slice(r0, r0 + 3 * BLOCK)
            pieces = [(kwin[rows, ca], kwin[rows, cb], vwin[rows, ca], vwin[rows, cb], mask),
                      (kx_ref[0, :, ca], kx_ref[0, :, cb], vx_ref[0, :, ca], vx_ref[0, :, cb], None)]
        else:
            pieces = [(kc_ref[0, :, ca], kc_ref[0, :, cb], vc_ref[0, :, ca], vc_ref[0, :, cb], None)]
        return lhs, pieces

    def unit_scores(ib, hk):
        lhs, pieces = unit_pieces(ib, hk)
        return _scores(lhs, pieces), pieces

    def unit_finish(ib, hk, scores, pieces):
        r0 = ib * BLOCK
        q0 = hk * Q_GROUP * HEAD_DIM
        h0 = hk * Q_GROUP
        sink_a = jnp.where(srow < BLOCK, sink_ref[layer, h0], sink_ref[layer, h0 + 2])
        sink_b = jnp.where(srow < BLOCK, sink_ref[layer, h0 + 1], sink_ref[layer, h0 + 3])
        o = _softmax_pv(scores, pieces, sink_a, sink_b).astype(BF16)
        attn[r0:r0 + BLOCK, q0:q0 + PAIR_WIDTH] = o[0:BLOCK]
        attn[r0:r0 + BLOCK, q0 + PAIR_WIDTH:q0 + 2 * PAIR_WIDTH] = o[BLOCK:2 * BLOCK]

    pbuf[0:POOL_HALO] = jnp.where(i > 0, pp_ref[0], 0.0)
    pbuf[POOL_HALO:POOL_HALO + tq] = pc_ref[0]
    pbuf[POOL_HALO + tq:2 * POOL_HALO + tq] = jnp.where(i < n_tiles - 1, pn_ref[0], 0.0)
    pool_d = [dg.astype(BF16) for dg in _pool_diff(pbuf, tq, seq_len, i * tq)]

    n_sub = tq // SUB_B
    qb_per_sub = SUB_B // BLOCK

    def slab(j):
        return slice(j * SUB_B, (j + 1) * SUB_B)

    def gate(j, col0):
        return _sigmoid(_dot(h[slab(j)], wg_ref[:, col0:col0 + d]))

    def pool_branch(j):
        pooled = []
        for gi, dg in enumerate(pool_d):
            cols = slice(gi * POOL_GROUP_DIM, (gi + 1) * POOL_GROUP_DIM)
            pooled.append(_dot(dg[slab(j)], wpool_ref[gi]) * psc_ref[:, cols])
        return _dot(jnp.concatenate(pooled, axis=-1).astype(BF16), wbp_ref[...])

    def merge(j, gate_attn, gate_pool, z_pool):
        r = slab(j)
        y = gate_attn * _dot(attn[r, :], wba_ref[...]) + gate_pool * z_pool
        o_ref[0, r, :] = x[r] + m[:, 2 * d:3 * d] * _dot(y.astype(BF16), wo_ref[...])

    units = [(ib, hk) for ib in range(n_qb) for hk in range(N_KV_HEADS)]
    units_per_sub = qb_per_sub * N_KV_HEADS
    dense = []
    for j in range(n_sub):
        dense += [functools.partial(gate, j, A_WIDTH), functools.partial(gate, j, A_WIDTH + d),
                  functools.partial(pool_branch, j)]
    assert units_per_sub >= 3
    dense_out = []
    pending = unit_scores(*units[0])
    for k, u in enumerate(units):
        nxt = unit_scores(*units[k + 1]) if k + 1 < len(units) else None
        if k < len(dense):
            dense_out.append(dense[k]())
        unit_finish(*u, *pending)
        pending = nxt
        if (k + 1) % units_per_sub == 0:
            j = k // units_per_sub
            assert len(dense_out) >= 3 * (j + 1)
            merge(j, *dense_out[3 * j:3 * j + 3])


def _stage_b(x, mods, mod_row, layer, norm_g, q, kz, vz, ctx_kv, p, sink, w_in, w_pool, pool_scale,
             w_br_attn, w_br_pool, w_out, *, tq):
    bsz, n, d = x.shape
    latent = ctx_kv is not None
    n_qb = tq // BLOCK
    nb = n // BLOCK
    tile = lambda b, i: (b, i, 0)
    in_specs = [
        pl.BlockSpec((1, tq, d), tile),
        pl.BlockSpec((1, 1, N_MOD * d), lambda b, i: (mod_row(b), 0, 0)),
        _layer_resident(layer, (1, d)),
        pl.BlockSpec((1, tq, Q_WIDTH), tile),
    ]
    args = [x, mods, norm_g, q]
    kv_cur = pl.BlockSpec((1, tq, KV_POS_WIDTH), tile)
    if latent:
        kv_prev = pl.BlockSpec((1, BLOCK, KV_POS_WIDTH), lambda b, i: (b, jnp.maximum(i * n_qb - 1, 0), 0))
        kv_next = pl.BlockSpec((1, BLOCK, KV_POS_WIDTH), lambda b, i: (b, jnp.minimum((i + 1) * n_qb, nb - 1), 0))
        n_ctx = ctx_kv[0].shape[1]
        kv_ctx = pl.BlockSpec((1, n_ctx, KV_POS_WIDTH), lambda b, i: (b, 0, 0))
        in_specs += [kv_prev, kv_cur, kv_next, kv_prev, kv_cur, kv_next, kv_ctx, kv_ctx]
        args += [kz, kz, kz, vz, vz, vz, ctx_kv[0], ctx_kv[1]]
    else:
        in_specs += [kv_cur, kv_cur]
        args += [kz, vz]
    n_halo = n // POOL_HALO
    per_tile = tq // POOL_HALO
    in_specs += [
        pl.BlockSpec((1, POOL_HALO, POOL_WIDTH), lambda b, i: (b, jnp.maximum(i * per_tile - 1, 0), 0)),
        pl.BlockSpec((1, tq, POOL_WIDTH), tile),
        pl.BlockSpec((1, POOL_HALO, POOL_WIDTH), lambda b, i: (b, jnp.minimum((i + 1) * per_tile, n_halo - 1), 0)),
        pl.BlockSpec(memory_space=pltpu.SMEM),
        _layer_resident(layer, (d, IN_WIDTH)),
        _layer_resident(layer, (N_POOL_GROUPS, POOL_GROUP_DIM, POOL_GROUP_DIM)),
        _layer_resident(layer, (1, POOL_WIDTH)),
        _layer_resident(layer, (Q_WIDTH, d)),
        _layer_resident(layer, (POOL_WIDTH, d)),
        _layer_resident(layer, (d, d)),
    ]
    args += [p, p, p, sink, w_in, w_pool, pool_scale, w_br_attn, w_br_pool, w_out]
    scratch = []
    scratch_bytes = 0
    if latent:
        scratch += [pltpu.VMEM((tq + 2 * BLOCK, KV_POS_WIDTH), BF16)] * 2
        scratch_bytes += 2 * (tq + 2 * BLOCK) * KV_POS_WIDTH * 2
    scratch += [pltpu.VMEM((tq + 2 * POOL_HALO, POOL_WIDTH), F32), pltpu.VMEM((tq, Q_WIDTH), BF16)]
    scratch_bytes += (tq + 2 * POOL_HALO) * POOL_WIDTH * 4 + tq * Q_WIDTH * 2
    pipelined = (2 * tq * d * 4 + tq * Q_WIDTH * 2 + 2 * (tq + 3 * BLOCK) * KV_POS_WIDTH * 2
                 + (tq + 2 * POOL_HALO) * POOL_WIDTH * 4)
    resident = (IN_WIDTH * d + Q_WIDTH * d + POOL_WIDTH * d + d * d) * 2
    return pl.pallas_call(
        functools.partial(_stage_b_kernel, tq=tq, seq_len=n, latent=latent, layer=layer),
        grid=(bsz, n // tq),
        in_specs=in_specs,
        out_specs=pl.BlockSpec((1, tq, d), tile),
        out_shape=jax.ShapeDtypeStruct((bsz, n, d), F32),
        scratch_shapes=scratch,
        compiler_params=pltpu.CompilerParams(
            dimension_semantics=("parallel", "parallel"),
            vmem_limit_bytes=_vmem_limit(pipelined, resident, scratch_bytes, 12 * SUB_B * d * 4 + 4 * tq * d * 4)),
        name="stage_b_latent" if latent else "stage_b_ctx",
    )(*args)


def _stage_c_kernel(x_ref, xp_ref, xn_ref, mod_ref, g_ref, wu_ref, cw_ref, cb_ref, wd_ref, o_ref,
                    hbuf, *, tm, seq_len):
    d = D_MODEL
    i = pl.program_id(1)
    n_tiles = seq_len // tm
    n_sub = tm // SUB_C
    m = mod_ref[0]
    g = g_ref[...]
    shift, scale = m[:, 3 * d:4 * d], m[:, 4 * d:5 * d]
    x = x_ref[0]
    hp = jnp.where(i > 0, _modulate(xp_ref[0], g, shift, scale), 0.0)
    hn = jnp.where(i < n_tiles - 1, _modulate(xn_ref[0], g, shift, scale), 0.0)
    h = _modulate(x, g, shift, scale)
    rows = SUB_C + 2 * CONV_HALO
    assert SUB_C % SUBLANES_BF16 == 0 and (2 * CONV_HALO) % SUBLANES_BF16 == 0
    for j in range(n_sub):
        r0 = j * SUB_C
        nxt = hn if j == n_sub - 1 else h[r0 + SUB_C:r0 + SUB_C + CONV_HALO]
        prv = hp if j == 0 else h[r0 - CONV_HALO:r0]
        hbuf[j, 0:SUB_C] = h[r0:r0 + SUB_C].astype(BF16)
        hbuf[j, SUB_C:rows] = jnp.concatenate([nxt, prv], axis=0).astype(BF16)

    def conv(u, cols):
        uc = cb_ref[:, cols] + pltpu.roll(u, 1, 0)[0:SUB_C] * cw_ref[0:1, cols]
        uc = uc + u[0:SUB_C] * cw_ref[1:2, cols]
        return uc + pltpu.roll(u, rows - 1, 0)[0:SUB_C] * cw_ref[2:3, cols]

    def conv_act(u, c):
        a = conv(u[0], slice(c * FF_CHUNK, (c + 1) * FF_CHUNK))
        b = conv(u[1], slice(D_FF + c * FF_CHUNK, D_FF + (c + 1) * FF_CHUNK))
        return ((a * _sigmoid(a)) * b).astype(BF16)

    def up(j, c):
        return [_dot(hbuf[j], wu_ref[:, off + c * FF_CHUNK:off + (c + 1) * FF_CHUNK]) for off in (0, D_FF)]

    acc = [None] * n_sub
    act_prev = [None] * n_sub
    u_queue = [[up(j, c) for c in range(UP_AHEAD)] for j in range(n_sub)]
    for c in range(N_FF_CHUNKS + 1):
        for j in range(n_sub):
            if c + UP_AHEAD < N_FF_CHUNKS:
                u_queue[j].append(up(j, c + UP_AHEAD))
            act = conv_act(u_queue[j].pop(0), c) if c < N_FF_CHUNKS else None
            if c >= 1:
                krows = slice((c - 1) * FF_CHUNK, c * FF_CHUNK)
                t = [_dot(act_prev[j], wd_ref[krows, k * (d // 2):(k + 1) * (d // 2)]) for k in range(2)]
                acc[j] = t if acc[j] is None else [a + b for a, b in zip(acc[j], t)]
            act_prev[j] = act
    for j in range(n_sub):
        r = slice(j * SUB_C, (j + 1) * SUB_C)
        o_ref[0, r, :] = x[r] + m[:, 5 * d:6 * d] * jnp.concatenate(acc[j], axis=-1)


def _stage_c(x, mods, mod_row, layer, norm_g, w_up, conv_w, conv_b, w_down, *, tm):
    bsz, n, d = x.shape
    tile = lambda b, i: (b, i, 0)
    per_tile = tm // CONV_HALO
    n_halo = n // CONV_HALO
    in_specs = [
        pl.BlockSpec((1, tm, d), tile),
        pl.BlockSpec((1, CONV_HALO, d), lambda b, i: (b, jnp.maximum(i * per_tile - 1, 0), 0)),
        pl.BlockSpec((1, CONV_HALO, d), lambda b, i: (b, jnp.minimum((i + 1) * per_tile, n_halo - 1), 0)),
        pl.BlockSpec((1, 1, N_MOD * d), lambda b, i: (mod_row(b), 0, 0)),
        _layer_resident(layer, (1, d)),
        _layer_resident(layer, (d, 2 * D_FF)),
        _layer_resident(layer, (CONV_WIDTH, 2 * D_FF)),
        _layer_resident(layer, (1, 2 * D_FF)),
        _layer_resident(layer, (D_FF, d)),
    ]
    assert tm % SUB_C == 0
    hbuf_shape = (tm // SUB_C, SUB_C + 2 * CONV_HALO, d)
    scratch_bytes = hbuf_shape[0] * hbuf_shape[1] * d * 2
    return pl.pallas_call(
        functools.partial(_stage_c_kernel, tm=tm, seq_len=n),
        grid=(bsz, n // tm),
        in_specs=in_specs,
        out_specs=pl.BlockSpec((1, tm, d), tile),
        out_shape=jax.ShapeDtypeStruct((bsz, n, d), F32),
        scratch_shapes=[pltpu.VMEM(hbuf_shape, BF16)],
        compiler_params=pltpu.CompilerParams(
            dimension_semantics=("parallel", "parallel"),
            vmem_limit_bytes=_vmem_limit(2 * tm * d * 4 + 2 * CONV_HALO * d * 4, 3 * d * D_FF * 2,
                                         scratch_bytes, 8 * tm * d * 4)),
        name="stage_c",
    )(x, x, x, mods, norm_g, w_up, conv_w, conv_b, w_down)


def _rope_tables(n_tok):
    rows = n_tok // GRID_W
    row = jnp.repeat(jnp.arange(rows, dtype=jnp.int32), GRID_W).astype(F32)
    col = jnp.tile(jnp.arange(GRID_W, dtype=jnp.int32), rows).astype(F32)
    n_freq = HEAD_DIM // 4
    inv = ROPE_THETA ** (-jnp.arange(n_freq, dtype=F32) / n_freq)
    ar, ac = row[:, None] * inv, col[:, None] * inv
    cos = jnp.concatenate([jnp.cos(ar), jnp.cos(ar), jnp.cos(ac), jnp.cos(ac)], axis=-1)
    sin = jnp.concatenate([-jnp.sin(ar), jnp.sin(ar), -jnp.sin(ac), jnp.sin(ac)], axis=-1)
    return jnp.tile(cos, (1, LANES // HEAD_DIM)), jnp.tile(sin, (1, LANES // HEAD_DIM))


def _swap_rope_halves(a):
    quarter = HEAD_DIM // 4
    upper = (jnp.arange(a.shape[-1]) & quarter) != 0
    return jnp.where(upper, jnp.roll(a, quarter, axis=-1), jnp.roll(a, -quarter, axis=-1))


def kernel(x, c, ctx, c_ctx, w_mod, b_mod, norm1_g, norm2_g, w_in, q_gain, k_gain, sink, w_pool, pool_scale,
           w_br_attn, w_br_pool, w_out, w_up, conv_w, conv_b, w_down):
    bsz, n_tok, d = x.shape
    n_ctx = ctx.shape[1]
    assert bsz + 1 <= MOD_ROWS
    cond = jnp.zeros((MOD_ROWS, d), F32).at[:bsz].set(c).at[bsz].set(c_ctx)
    mods = _adaln(cond, w_mod, b_mod).reshape(DEPTH * MOD_ROWS, 1, N_MOD * d)

    w_in_b = w_in.astype(BF16)
    g1 = norm1_g.reshape(DEPTH, 1, d)
    g2 = norm2_g.reshape(DEPTH, 1, d)
    qg = jnp.tile(q_gain, (1, LANES // HEAD_DIM)).reshape(DEPTH, 1, LANES)
    kg = jnp.tile(k_gain, (1, LANES // HEAD_DIM)).reshape(DEPTH, 1, LANES)
    rope_args = (_swap_rope_halves(w_in_b[:, :, :Q_WIDTH + KV_WIDTH]),
                 _swap_rope_halves(qg), _swap_rope_halves(kg)) + _rope_tables(n_tok)
    b_weights = (sink, w_in_b, w_pool.astype(BF16), pool_scale.reshape(DEPTH, 1, POOL_WIDTH),
                 w_br_attn.astype(BF16), w_br_pool.astype(BF16), w_out.astype(BF16))
    c_weights = (g2, w_up.astype(BF16), conv_w, conv_b.reshape(DEPTH, 1, 2 * D_FF), w_down.astype(BF16))

    xc = ctx
    for l in range(DEPTH):
        last = l == DEPTH - 1
        lat_row = lambda b, l=l: l * MOD_ROWS + b
        ctx_row = lambda b, l=l: l * MOD_ROWS + bsz
        qc, kzc, vzc, pc = _stage_a(xc, mods, ctx_row, l, g1, w_in_b, qg, kg, None, tm=n_ctx)
        q, kz, vz, p = _stage_a(x, mods, lat_row, l, g1, w_in_b, qg, kg, rope_args, tm=TM_A)
        x = _stage_b(x, mods, lat_row, l, g1, q, kz, vz, (kzc, vzc), p, *b_weights, tq=TQ_B)
        x = _stage_c(x, mods, lat_row, l, *c_weights, tm=TM_C)
        if not last:
            xc = _stage_b(xc, mods, ctx_row, l, g1, qc, kzc, vzc, None, pc, *b_weights, tq=n_ctx)
            xc = _stage_c(xc, mods, ctx_row, l, *c_weights, tm=n_ctx)
    return x
```

```python
import functools

import jax
import jax.numpy as jnp
from jax import lax
from jax.experimental import pallas as pl
from jax.experimental.pallas import tpu as pltpu

D_MODEL = 1024
DEPTH = 4
GRID_W = 64
N_Q_HEADS = 8
N_KV_HEADS = 2
HEAD_DIM = 64
Q_GROUP = N_Q_HEADS // N_KV_HEADS
WINDOW = 128
BLOCK = 128
ROPE_THETA = 10000.0
POOL_SIZES = (2, 4, 8, 16)
N_POOL_GROUPS = len(POOL_SIZES)
POOL_GROUP_DIM = D_MODEL // 8
POOL_WIDTH = N_POOL_GROUPS * POOL_GROUP_DIM
Q_WIDTH = N_Q_HEADS * HEAD_DIM
KV_WIDTH = N_KV_HEADS * HEAD_DIM
D_FF = 2816
CONV_WIDTH = 3
N_MOD = 6
EPS = 1e-6
NEG_INF = -1e30

LANES = 128
SUBLANES_F32 = 8
SUBLANES_BF16 = 16
VMEM_BYTES_V7X = 64 << 20

KV_POS_WIDTH = 2 * N_KV_HEADS * LANES
PAIR_WIDTH = 2 * HEAD_DIM
A_WIDTH = Q_WIDTH + 2 * KV_WIDTH + POOL_WIDTH
IN_WIDTH = A_WIDTH + 2 * D_MODEL
POOL_HALO = SUBLANES_F32
DOUBLING_MIN_WIDTH = 8
CONV_HALO = SUBLANES_F32
UP_AHEAD = 3
FF_CHUNK = 256
N_FF_CHUNKS = D_FF // FF_CHUNK
MOD_ROWS = 16
TM_A, TQ_B, TM_C = 1024, 512, 512
SUB_A = SUB_B = SUB_C = 256

F32 = jnp.float32
BF16 = jnp.bfloat16


LOG2_E = 1.4426950408889634


def _sigmoid(v):
    return 1.0 / (1.0 + jnp.exp2(v * (-LOG2_E)))


def _modulate(x, g, shift, scale):
    ms = jnp.mean(x * x, axis=-1, keepdims=True)
    y = x * lax.rsqrt(ms + EPS)
    return (y * g) * (1.0 + scale) + shift


def _dot(a, b):
    return jnp.dot(a, b, preferred_element_type=F32)


def _dot_t(a, b):
    return lax.dot_general(a, b, (((1,), (1,)), ((), ())), preferred_element_type=F32)


def _vmem_limit(pipelined_bytes, resident_bytes, scratch_bytes, temp_bytes):
    need = 2 * pipelined_bytes + 2 * resident_bytes + scratch_bytes + temp_bytes
    assert need < VMEM_BYTES_V7X, need
    return int(need)


def _layer_resident(layer, shape):
    return pl.BlockSpec((None,) + tuple(shape), lambda *_: (layer,) + (0,) * len(shape))


def _adaln_kernel(c_ref, w_ref, b_ref, o_ref):
    c = c_ref[...]
    s = c * _sigmoid(c)
    o_ref[0] = jnp.dot(s, w_ref[0], preferred_element_type=F32,
                       precision=lax.Precision.HIGHEST) + b_ref[0]


def _adaln(cond, w_mod, b_mod):
    n_out = N_MOD * D_MODEL
    nc = n_out // 2
    return pl.pallas_call(
        _adaln_kernel,
        grid=(DEPTH, n_out // nc),
        in_specs=[
            pl.BlockSpec((MOD_ROWS, D_MODEL), lambda l, j: (0, 0)),
            pl.BlockSpec((1, D_MODEL, nc), lambda l, j: (l, 0, j)),
            pl.BlockSpec((1, 1, nc), lambda l, j: (l, 0, j)),
        ],
        out_specs=pl.BlockSpec((1, MOD_ROWS, nc), lambda l, j: (l, 0, j)),
        out_shape=jax.ShapeDtypeStruct((DEPTH, MOD_ROWS, n_out), F32),
        compiler_params=pltpu.CompilerParams(
            dimension_semantics=("parallel", "parallel"),
            vmem_limit_bytes=_vmem_limit(D_MODEL * nc * 4 + MOD_ROWS * nc * 8, MOD_ROWS * D_MODEL * 4, 0, 4 << 20)),
        name="adaln",
    )(cond, w_mod, b_mod.reshape(DEPTH, 1, n_out))


def _head_norm_rope(z, gain, rope, out_scale):
    low = lax.broadcasted_iota(jnp.int32, z.shape, 1) < HEAD_DIM
    sq = z * z
    ss_lo = jnp.sum(jnp.where(low, sq, 0.0), axis=-1, keepdims=True)
    ss_hi = jnp.sum(jnp.where(low, 0.0, sq), axis=-1, keepdims=True)
    r_lo = lax.rsqrt(ss_lo * (1.0 / HEAD_DIM) + EPS)
    r_hi = lax.rsqrt(ss_hi * (1.0 / HEAD_DIM) + EPS)
    r = jnp.where(low, r_lo, r_hi)
    t = z * gain
    if rope is not None:
        z_rot, gain_rot, cos, sin = rope
        t = t * cos + (z_rot * gain_rot) * sin
    t = t * r
    if out_scale != 1.0:
        t = t * out_scale
    return t


def _stage_a_kernel(*refs, rope):
    if rope:
        (x_ref, mod_ref, g_ref, w_ref, qg_ref, kg_ref, wr_ref, qgr_ref, kgr_ref, cos_ref, sin_ref,
         q_ref, kz_ref, vz_ref, p_ref) = refs
    else:
        x_ref, mod_ref, g_ref, w_ref, qg_ref, kg_ref, q_ref, kz_ref, vz_ref, p_ref = refs
    d = D_MODEL
    m = mod_ref[0]
    qg, kg = qg_ref[...], kg_ref[...]
    k0 = Q_WIDTH
    p0 = k0 + 2 * KV_WIDTH
    tm = x_ref.shape[1]
    sub = min(tm, SUB_A)

    def project(r):
        h = _modulate(x_ref[0, r, :], g_ref[...], m[:, 0:d], m[:, d:2 * d]).astype(BF16)
        z = {"q": _dot(h, w_ref[:, 0:Q_WIDTH]), "kv": _dot(h, w_ref[:, k0:p0])}
        if rope:
            z["q_rot"] = _dot(h, wr_ref[:, 0:Q_WIDTH])
            z["k_rot"] = _dot(h, wr_ref[:, Q_WIDTH:Q_WIDTH + KV_WIDTH])
        p_ref[0, r, :] = _dot(h, w_ref[:, p0:p0 + POOL_WIDTH])
        return z

    def finish(r, z):
        for c in range(Q_WIDTH // LANES):
            sl = slice(c * LANES, (c + 1) * LANES)
            rp = (z["q_rot"][:, sl], qgr_ref[...], cos_ref[r, :], sin_ref[r, :]) if rope else None
            q_ref[0, r, sl] = _head_norm_rope(z["q"][:, sl], qg, rp, HEAD_DIM ** -0.5).astype(BF16)
        rp = (z["k_rot"], kgr_ref[...], cos_ref[r, :], sin_ref[r, :]) if rope else None
        kk = _head_norm_rope(z["kv"][:, 0:KV_WIDTH], kg, rp, 1.0)
        vv = z["kv"][:, KV_WIDTH:2 * KV_WIDTH]
        low = lax.broadcasted_iota(jnp.int32, kk.shape, 1) < HEAD_DIM
        for src, dst in ((kk, kz_ref), (vv, vz_ref)):
            swapped = pltpu.roll(src, HEAD_DIM, 1)
            dst[0, r, 0 * LANES:1 * LANES] = jnp.where(low, src, 0.0).astype(BF16)
            dst[0, r, 1 * LANES:2 * LANES] = jnp.where(low, 0.0, swapped).astype(BF16)
            dst[0, r, 2 * LANES:3 * LANES] = jnp.where(low, swapped, 0.0).astype(BF16)
            dst[0, r, 3 * LANES:4 * LANES] = jnp.where(low, 0.0, src).astype(BF16)

    slabs = [slice(j * sub, (j + 1) * sub) for j in range(tm // sub)]
    pending = None
    for r in slabs:
        z = project(r)
        if pending is not None:
            finish(*pending)
        pending = (r, z)
    finish(*pending)


def _stage_a(x, mods, mod_row, layer, norm_g, w_in, q_gain, k_gain, rope_args, *, tm):
    bsz, n, d = x.shape
    wa = A_WIDTH
    rope = rope_args is not None
    in_specs = [
        pl.BlockSpec((1, tm, d), lambda i, b: (b, i, 0)),
        pl.BlockSpec((1, 1, N_MOD * d), lambda i, b: (mod_row(b), 0, 0)),
        _layer_resident(layer, (1, d)),
        _layer_resident(layer, (d, wa)),
        _layer_resident(layer, (1, LANES)),
        _layer_resident(layer, (1, LANES)),
    ]
    args = [x, mods, norm_g, w_in, q_gain, k_gain]
    if rope:
        w_rot, q_gain_rot, k_gain_rot, cos, sin = rope_args
        in_specs += [_layer_resident(layer, (d, Q_WIDTH + KV_WIDTH)), _layer_resident(layer, (1, LANES)),
                     _layer_resident(layer, (1, LANES))]
        in_specs += [pl.BlockSpec((tm, LANES), lambda i, b: (i, 0))] * 2
        args += [w_rot, q_gain_rot, k_gain_rot, cos, sin]
    out_shape = [
        jax.ShapeDtypeStruct((bsz, n, Q_WIDTH), BF16),
        jax.ShapeDtypeStruct((bsz, n, KV_POS_WIDTH), BF16),
        jax.ShapeDtypeStruct((bsz, n, KV_POS_WIDTH), BF16),
        jax.ShapeDtypeStruct((bsz, n, POOL_WIDTH), F32),
    ]
    out_specs = [pl.BlockSpec((1, tm, s.shape[2]), lambda i, b: (b, i, 0)) for s in out_shape]
    pipelined = tm * d * 4 + tm * (Q_WIDTH + 2 * KV_POS_WIDTH) * 2 + tm * POOL_WIDTH * 4 + 2 * tm * LANES * 4
    return pl.pallas_call(
        functools.partial(_stage_a_kernel, rope=rope),
        grid=(n // tm, bsz),
        in_specs=in_specs,
        out_specs=out_specs,
        out_shape=out_shape,
        compiler_params=pltpu.CompilerParams(
            dimension_semantics=("parallel", "parallel"),
            vmem_limit_bytes=_vmem_limit(pipelined, d * (wa + Q_WIDTH + KV_WIDTH) * 2, 0, 8 * tm * d * 4)),
        name="stage_a_rope" if rope else "stage_a",
    )(*args)


def _pool_diff(pbuf, tq, seq_len, tile_start):
    pos = tile_start + lax.broadcasted_iota(jnp.int32, (tq, POOL_GROUP_DIM), 0)
    rows = tq + 2 * POOL_HALO

    def window_sums(gi, w):
        cols = slice(gi * POOL_GROUP_DIM, (gi + 1) * POOL_GROUP_DIM)
        if w < DOUBLING_MIN_WIDTH:
            acc = pbuf[pl.ds(POOL_HALO - w // 2, tq), cols]
            for j in range(1, w):
                acc = acc + pbuf[pl.ds(POOL_HALO - w // 2 + j, tq), cols]
            return acc
        s = pbuf[...][:, cols]
        span = 1
        while span < w:
            s = s + pltpu.roll(s, rows - span, 0)
            span *= 2
        start = POOL_HALO - w // 2
        return s[start:start + tq]

    outs = []
    for gi, w in enumerate(POOL_SIZES):
        cols = slice(gi * POOL_GROUP_DIM, (gi + 1) * POOL_GROUP_DIM)
        cnt = (jnp.minimum(pos + (w - w // 2), seq_len) - jnp.maximum(pos - w // 2, 0)).astype(F32)
        outs.append(window_sums(gi, w) / cnt - pbuf[pl.ds(POOL_HALO, tq), cols])
    return outs


def _scores(lhs, pieces):
    out = []
    for idx in (0, 1):
        s = []
        for pc in pieces:
            sc = _dot_t(lhs, pc[idx])
            if pc[4] is not None:
                sc = jnp.where(pc[4], sc, NEG_INF)
            s.append(sc)
        out.append(s)
    return out


def _softmax_pv(scores, pieces, sink_a, sink_b):
    def side(s, sink):
        m = sink
        for sc in s:
            m = jnp.maximum(m, jnp.max(sc, axis=-1, keepdims=True))
        e = [jnp.exp(sc - m) for sc in s]
        den = jnp.sum(e[0], axis=-1, keepdims=True)
        for ee in e[1:]:
            den = den + jnp.sum(ee, axis=-1, keepdims=True)
        den = den + jnp.exp(sink - m)
        return e, den

    e_a, den_a = side(scores[0], sink_a)
    e_b, den_b = side(scores[1], sink_b)
    o = None
    for pc, ea, eb in zip(pieces, e_a, e_b):
        t = _dot(ea.astype(BF16), pc[2]) + _dot(eb.astype(BF16), pc[3])
        o = t if o is None else o + t
    lane = lax.broadcasted_iota(jnp.int32, o.shape, 1)
    return o / jnp.where(lane < HEAD_DIM, den_a, den_b)


def _stage_b_kernel(*refs, tq, seq_len, latent, layer):
    if latent:
        (x_ref, mod_ref, g_ref, q_ref, kp_ref, kc_ref, kn_ref, vp_ref, vc_ref, vn_ref, kx_ref, vx_ref,
         pp_ref, pc_ref, pn_ref, sink_ref, wg_ref, wpool_ref, psc_ref, wba_ref, wbp_ref, wo_ref,
         o_ref, kwin, vwin, pbuf, attn) = refs
    else:
        (x_ref, mod_ref, g_ref, q_ref, kc_ref, vc_ref,
         pp_ref, pc_ref, pn_ref, sink_ref, wg_ref, wpool_ref, psc_ref, wba_ref, wbp_ref, wo_ref,
         o_ref, pbuf, attn) = refs
    d = D_MODEL
    i = pl.program_id(1)
    n_tiles = seq_len // tq
    n_qb = tq // BLOCK
    n_blocks = seq_len // BLOCK

    x = x_ref[0]
    m = mod_ref[0]
    h = _modulate(x, g_ref[...], m[:, 0:d], m[:, d:2 * d]).astype(BF16)

    if latent:
        kwin[0:BLOCK] = kp_ref[0]
        kwin[BLOCK:BLOCK + tq] = kc_ref[0]
        kwin[BLOCK + tq:2 * BLOCK + tq] = kn_ref[0]
        vwin[0:BLOCK] = vp_ref[0]
        vwin[BLOCK:BLOCK + tq] = vc_ref[0]
        vwin[BLOCK + tq:2 * BLOCK + tq] = vn_ref[0]
        row = lax.broadcasted_iota(jnp.int32, (2 * BLOCK, 3 * BLOCK), 0) & (BLOCK - 1)
        col = lax.broadcasted_iota(jnp.int32, (2 * BLOCK, 3 * BLOCK), 1)
        band = (col >= row) & (col <= row + 2 * WINDOW)
    srow = lax.broadcasted_iota(jnp.int32, (2 * BLOCK, 1), 0)

    def unit_pieces(ib, hk):
        r0 = ib * BLOCK
        q0 = hk * Q_GROUP * HEAD_DIM
        lhs = jnp.concatenate([q_ref[0, r0:r0 + BLOCK, q0:q0 + PAIR_WIDTH],
                               q_ref[0, r0:r0 + BLOCK, q0 + PAIR_WIDTH:q0 + 2 * PAIR_WIDTH]], axis=0)
        ca = slice((2 * hk) * LANES, (2 * hk + 1) * LANES)
        cb = slice((2 * hk + 1) * LANES, (2 * hk + 2) * LANES)
        if latent:
            g = i * n_qb + ib
            lo = jnp.where(g > 0, 0, BLOCK)
            hi = jnp.where(g < n_blocks - 1, 3 * BLOCK, 2 * BLOCK)
            mask = band & (col >= lo) & (col < hi)
            rows = slice(r0, r0 + 3 * BLOCK)
            pieces = [(kwin[rows, ca], kwin[rows, cb], vwin[rows, ca], vwin[rows, cb], mask),
                      (kx_ref[0, :, ca], kx_ref[0, :, cb], vx_ref[0, :, ca], vx_ref[0, :, cb], None)]
        else:
            pieces = [(kc_ref[0, :, ca], kc_ref[0, :, cb], vc_ref[0, :, ca], vc_ref[0, :, cb], None)]
        return lhs, pieces

    def unit_scores(ib, hk):
        lhs, pieces = unit_pieces(ib, hk)
        return _scores(lhs, pieces), pieces

    def unit_finish(ib, hk, scores, pieces):
        r0 = ib * BLOCK
        q0 = hk * Q_GROUP * HEAD_DIM
        h0 = hk * Q_GROUP
        sink_a = jnp.where(srow < BLOCK, sink_ref[layer, h0], sink_ref[layer, h0 + 2])
        sink_b = jnp.where(srow < BLOCK, sink_ref[layer, h0 + 1], sink_ref[layer, h0 + 3])
        o = _softmax_pv(scores, pieces, sink_a, sink_b).astype(BF16)
        attn[r0:r0 + BLOCK, q0:q0 + PAIR_WIDTH] = o[0:BLOCK]
        attn[r0:r0 + BLOCK, q0 + PAIR_WIDTH:q0 + 2 * PAIR_WIDTH] = o[BLOCK:2 * BLOCK]

    pbuf[0:POOL_HALO] = jnp.where(i > 0, pp_ref[0], 0.0)
    pbuf[POOL_HALO:POOL_HALO + tq] = pc_ref[0]
    pbuf[POOL_HALO + tq:2 * POOL_HALO + tq] = jnp.where(i < n_tiles - 1, pn_ref[0], 0.0)
    pool_d = [dg.astype(BF16) for dg in _pool_diff(pbuf, tq, seq_len, i * tq)]

    n_sub = tq // SUB_B
    qb_per_sub = SUB_B // BLOCK

    def slab(j):
        return slice(j * SUB_B, (j + 1) * SUB_B)

    def gate(j, col0):
        return _sigmoid(_dot(h[slab(j)], wg_ref[:, col0:col0 + d]))

    def pool_branch(j):
        pooled = []
        for gi, dg in enumerate(pool_d):
            cols = slice(gi * POOL_GROUP_DIM, (gi + 1) * POOL_GROUP_DIM)
            pooled.append(_dot(dg[slab(j)], wpool_ref[gi]) * psc_ref[:, cols])
        return _dot(jnp.concatenate(pooled, axis=-1).astype(BF16), wbp_ref[...])

    def merge(j, gate_attn, gate_pool, z_pool):
        r = slab(j)
        y = gate_attn * _dot(attn[r, :], wba_ref[...]) + gate_pool * z_pool
        o_ref[0, r, :] = x[r] + m[:, 2 * d:3 * d] * _dot(y.astype(BF16), wo_ref[...])

    units = [(ib, hk) for ib in range(n_qb) for hk in range(N_KV_HEADS)]
    units_per_sub = qb_per_sub * N_KV_HEADS
    dense = []
    for j in range(n_sub):
        dense += [functools.partial(gate, j, A_WIDTH), functools.partial(gate, j, A_WIDTH + d),
                  functools.partial(pool_branch, j)]
    assert units_per_sub >= 3
    dense_out = []
    pending = unit_scores(*units[0])
    for k, u in enumerate(units):
        nxt = unit_scores(*units[k + 1]) if k + 1 < len(units) else None
        if k < len(dense):
            dense_out.append(dense[k]())
        unit_finish(*u, *pending)
        pending = nxt
        if (k + 1) % units_per_sub == 0:
            j = k // units_per_sub
            assert len(dense_out) >= 3 * (j + 1)
            merge(j, *dense_out[3 * j:3 * j + 3])


def _stage_b(x, mods, mod_row, layer, norm_g, q, kz, vz, ctx_kv, p, sink, w_in, w_pool, pool_scale,
             w_br_attn, w_br_pool, w_out, *, tq):
    bsz, n, d = x.shape
    latent = ctx_kv is not None
    n_qb = tq // BLOCK
    nb = n // BLOCK
    tile = lambda b, i: (b, i, 0)
    in_specs = [
        pl.BlockSpec((1, tq, d), tile),
        pl.BlockSpec((1, 1, N_MOD * d), lambda b, i: (mod_row(b), 0, 0)),
        _layer_resident(layer, (1, d)),
        pl.BlockSpec((1, tq, Q_WIDTH), tile),
    ]
    args = [x, mods, norm_g, q]
    kv_cur = pl.BlockSpec((1, tq, KV_POS_WIDTH), tile)
    if latent:
        kv_prev = pl.BlockSpec((1, BLOCK, KV_POS_WIDTH), lambda b, i: (b, jnp.maximum(i * n_qb - 1, 0), 0))
        kv_next = pl.BlockSpec((1, BLOCK, KV_POS_WIDTH), lambda b, i: (b, jnp.minimum((i + 1) * n_qb, nb - 1), 0))
        n_ctx = ctx_kv[0].shape[1]
        kv_ctx = pl.BlockSpec((1, n_ctx, KV_POS_WIDTH), lambda b, i: (b, 0, 0))
        in_specs += [kv_prev, kv_cur, kv_next, kv_prev, kv_cur, kv_next, kv_ctx, kv_ctx]
        args += [kz, kz, kz, vz, vz, vz, ctx_kv[0], ctx_kv[1]]
    else:
        in_specs += [kv_cur, kv_cur]
        args += [kz, vz]
    n_halo = n // POOL_HALO
    per_tile = tq // POOL_HALO
    in_specs += [
        pl.BlockSpec((1, POOL_HALO, POOL_WIDTH), lambda b, i: (b, jnp.maximum(i * per_tile - 1, 0), 0)),
        pl.BlockSpec((1, tq, POOL_WIDTH), tile),
        pl.BlockSpec((1, POOL_HALO, POOL_WIDTH), lambda b, i: (b, jnp.minimum((i + 1) * per_tile, n_halo - 1), 0)),
        pl.BlockSpec(memory_space=pltpu.SMEM),
        _layer_resident(layer, (d, IN_WIDTH)),
        _layer_resident(layer, (N_POOL_GROUPS, POOL_GROUP_DIM, POOL_GROUP_DIM)),
        _layer_resident(layer, (1, POOL_WIDTH)),
        _layer_resident(layer, (Q_WIDTH, d)),
        _layer_resident(layer, (POOL_WIDTH, d)),
        _layer_resident(layer, (d, d)),
    ]
    args += [p, p, p, sink, w_in, w_pool, pool_scale, w_br_attn, w_br_pool, w_out]
    scratch = []
    scratch_bytes = 0
    if latent:
        scratch += [pltpu.VMEM((tq + 2 * BLOCK, KV_POS_WIDTH), BF16)] * 2
        scratch_bytes += 2 * (tq + 2 * BLOCK) * KV_POS_WIDTH * 2
    scratch += [pltpu.VMEM((tq + 2 * POOL_HALO, POOL_WIDTH), F32), pltpu.VMEM((tq, Q_WIDTH), BF16)]
    scratch_bytes += (tq + 2 * POOL_HALO) * POOL_WIDTH * 4 + tq * Q_WIDTH * 2
    pipelined = (2 * tq * d * 4 + tq * Q_WIDTH * 2 + 2 * (tq + 3 * BLOCK) * KV_POS_WIDTH * 2
                 + (tq + 2 * POOL_HALO) * POOL_WIDTH * 4)
    resident = (IN_WIDTH * d + Q_WIDTH * d + POOL_WIDTH * d + d * d) * 2
    return pl.pallas_call(
        functools.partial(_stage_b_kernel, tq=tq, seq_len=n, latent=latent, layer=layer),
        grid=(bsz, n // tq),
        in_specs=in_specs,
        out_specs=pl.BlockSpec((1, tq, d), tile),
        out_shape=jax.ShapeDtypeStruct((bsz, n, d), F32),
        scratch_shapes=scratch,
        compiler_params=pltpu.CompilerParams(
            dimension_semantics=("parallel", "parallel"),
            vmem_limit_bytes=_vmem_limit(pipelined, resident, scratch_bytes, 12 * SUB_B * d * 4 + 4 * tq * d * 4)),
        name="stage_b_latent" if latent else "stage_b_ctx",
    )(*args)


def _stage_c_kernel(x_ref, xp_ref, xn_ref, mod_ref, g_ref, wu_ref, cw_ref, cb_ref, wd_ref, o_ref,
                    hbuf, *, tm, seq_len):
    d = D_MODEL
    i = pl.program_id(1)
    n_tiles = seq_len // tm
    n_sub = tm // SUB_C
    m = mod_ref[0]
    g = g_ref[...]
    shift, scale = m[:, 3 * d:4 * d], m[:, 4 * d:5 * d]
    x = x_ref[0]
    hp = jnp.where(i > 0, _modulate(xp_ref[0], g, shift, scale), 0.0)
    hn = jnp.where(i < n_tiles - 1, _modulate(xn_ref[0], g, shift, scale), 0.0)
    h = _modulate(x, g, shift, scale)
    rows = SUB_C + 2 * CONV_HALO
    assert SUB_C % SUBLANES_BF16 == 0 and (2 * CONV_HALO) % SUBLANES_BF16 == 0
    for j in range(n_sub):
        r0 = j * SUB_C
        nxt = hn if j == n_sub - 1 else h[r0 + SUB_C:r0 + SUB_C + CONV_HALO]
        prv = hp if j == 0 else h[r0 - CONV_HALO:r0]
        hbuf[j, 0:SUB_C] = h[r0:r0 + SUB_C].astype(BF16)
        hbuf[j, SUB_C:rows] = jnp.concatenate([nxt, prv], axis=0).astype(BF16)

    def conv(u, cols):
        uc = cb_ref[:, cols] + pltpu.roll(u, 1, 0)[0:SUB_C] * cw_ref[0:1, cols]
        uc = uc + u[0:SUB_C] * cw_ref[1:2, cols]
        return uc + pltpu.roll(u, rows - 1, 0)[0:SUB_C] * cw_ref[2:3, cols]

    def conv_act(u, c):
        a = conv(u[0], slice(c * FF_CHUNK, (c + 1) * FF_CHUNK))
        b = conv(u[1], slice(D_FF + c * FF_CHUNK, D_FF + (c + 1) * FF_CHUNK))
        return ((a * _sigmoid(a)) * b).astype(BF16)

    def up(j, c):
        return [_dot(hbuf[j], wu_ref[:, off + c * FF_CHUNK:off + (c + 1) * FF_CHUNK]) for off in (0, D_FF)]

    acc = [None] * n_sub
    act_prev = [None] * n_sub
    u_queue = [[up(j, c) for c in range(UP_AHEAD)] for j in range(n_sub)]
    for c in range(N_FF_CHUNKS + 1):
        for j in range(n_sub):
            if c + UP_AHEAD < N_FF_CHUNKS:
                u_queue[j].append(up(j, c + UP_AHEAD))
            act = conv_act(u_queue[j].pop(0), c) if c < N_FF_CHUNKS else None
            if c >= 1:
                krows = slice((c - 1) * FF_CHUNK, c * FF_CHUNK)
                t = [_dot(act_prev[j], wd_ref[krows, k * (d // 2):(k + 1) * (d // 2)]) for k in range(2)]
                acc[j] = t if acc[j] is None else [a + b for a, b in zip(acc[j], t)]
            act_prev[j] = act
    for j in range(n_sub):
        r = slice(j * SUB_C, (j + 1) * SUB_C)
        o_ref[0, r, :] = x[r] + m[:, 5 * d:6 * d] * jnp.concatenate(acc[j], axis=-1)


def _stage_c(x, mods, mod_row, layer, norm_g, w_up, conv_w, conv_b, w_down, *, tm):
    bsz, n, d = x.shape
    tile = lambda b, i: (b, i, 0)
    per_tile = tm // CONV_HALO
    n_halo = n // CONV_HALO
    in_specs = [
        pl.BlockSpec((1, tm, d), tile),
        pl.BlockSpec((1, CONV_HALO, d), lambda b, i: (b, jnp.maximum(i * per_tile - 1, 0), 0)),
        pl.BlockSpec((1, CONV_HALO, d), lambda b, i: (b, jnp.minimum((i + 1) * per_tile, n_halo - 1), 0)),
        pl.BlockSpec((1, 1, N_MOD * d), lambda b, i: (mod_row(b), 0, 0)),
        _layer_resident(layer, (1, d)),
        _layer_resident(layer, (d, 2 * D_FF)),
        _layer_resident(layer, (CONV_WIDTH, 2 * D_FF)),
        _layer_resident(layer, (1, 2 * D_FF)),
        _layer_resident(layer, (D_FF, d)),
    ]
    assert tm % SUB_C == 0
    hbuf_shape = (tm // SUB_C, SUB_C + 2 * CONV_HALO, d)
    scratch_bytes = hbuf_shape[0] * hbuf_shape[1] * d * 2
    return pl.pallas_call(
        functools.partial(_stage_c_kernel, tm=tm, seq_len=n),
        grid=(bsz, n // tm),
        in_specs=in_specs,
        out_specs=pl.BlockSpec((1, tm, d), tile),
        out_shape=jax.ShapeDtypeStruct((bsz, n, d), F32),
        scratch_shapes=[pltpu.VMEM(hbuf_shape, BF16)],
        compiler_params=pltpu.CompilerParams(
            dimension_semantics=("parallel", "parallel"),
            vmem_limit_bytes=_vmem_limit(2 * tm * d * 4 + 2 * CONV_HALO * d * 4, 3 * d * D_FF * 2,
                                         scratch_bytes, 8 * tm * d * 4)),
        name="stage_c",
    )(x, x, x, mods, norm_g, w_up, conv_w, conv_b, w_down)


def _rope_tables(n_tok):
    rows = n_tok // GRID_W
    row = jnp.repeat(jnp.arange(rows, dtype=jnp.int32), GRID_W).astype(F32)
    col = jnp.tile(jnp.arange(GRID_W, dtype=jnp.int32), rows).astype(F32)
    n_freq = HEAD_DIM // 4
    inv = ROPE_THETA ** (-jnp.arange(n_freq, dtype=F32) / n_freq)
    ar, ac = row[:, None] * inv, col[:, None] * inv
    cos = jnp.concatenate([jnp.cos(ar), jnp.cos(ar), jnp.cos(ac), jnp.cos(ac)], axis=-1)
    sin = jnp.concatenate([-jnp.sin(ar), jnp.sin(ar), -jnp.sin(ac), jnp.sin(ac)], axis=-1)
    return jnp.tile(cos, (1, LANES // HEAD_DIM)), jnp.tile(sin, (1, LANES // HEAD_DIM))


def _swap_rope_halves(a):
    quarter = HEAD_DIM // 4
    upper = (jnp.arange(a.shape[-1]) & quarter) != 0
    return jnp.where(upper, jnp.roll(a, quarter, axis=-1), jnp.roll(a, -quarter, axis=-1))


def kernel(x, c, ctx, c_ctx, w_mod, b_mod, norm1_g, norm2_g, w_in, q_gain, k_gain, sink, w_pool, pool_scale,
           w_br_attn, w_br_pool, w_out, w_up, conv_w, conv_b, w_down):
    bsz, n_tok, d = x.shape
    n_ctx = ctx.shape[1]
    assert bsz + 1 <= MOD_ROWS
    cond = jnp.zeros((MOD_ROWS, d), F32).at[:bsz].set(c).at[bsz].set(c_ctx)
    mods = _adaln(cond, w_mod, b_mod).reshape(DEPTH * MOD_ROWS, 1, N_MOD * d)

    w_in_b = w_in.astype(BF16)
    g1 = norm1_g.reshape(DEPTH, 1, d)
    g2 = norm2_g.reshape(DEPTH, 1, d)
    qg = jnp.tile(q_gain, (1, LANES // HEAD_DIM)).reshape(DEPTH, 1, LANES)
    kg = jnp.tile(k_gain, (1, LANES // HEAD_DIM)).reshape(DEPTH, 1, LANES)
    rope_args = (_swap_rope_halves(w_in_b[:, :, :Q_WIDTH + KV_WIDTH]),
                 _swap_rope_halves(qg), _swap_rope_halves(kg)) + _rope_tables(n_tok)
    b_weights = (sink, w_in_b, w_pool.astype(BF16), pool_scale.reshape(DEPTH, 1, POOL_WIDTH),
                 w_br_attn.astype(BF16), w_br_pool.astype(BF16), w_out.astype(BF16))
    c_weights = (g2, w_up.astype(BF16), conv_w, conv_b.reshape(DEPTH, 1, 2 * D_FF), w_down.astype(BF16))

    xc = ctx
    for l in range(DEPTH):
        last = l == DEPTH - 1
        lat_row = lambda b, l=l: l * MOD_ROWS + b
        ctx_row = lambda b, l=l: l * MOD_ROWS + bsz
        qc, kzc, vzc, pc = _stage_a(xc, mods, ctx_row, l, g1, w_in_b, qg, kg, None, tm=n_ctx)
        q, kz, vz, p = _stage_a(x, mods, lat_row, l, g1, w_in_b, qg, kg, rope_args, tm=TM_A)
        x = _stage_b(x, mods, lat_row, l, g1, q, kz, vz, (kzc, vzc), p, *b_weights, tq=TQ_B)
        x = _stage_c(x, mods, lat_row, l, *c_weights, tm=TM_C)
        if not last:
            xc = _stage_b(xc, mods, ctx_row, l, g1, qc, kzc, vzc, None, pc, *b_weights, tq=n_ctx)
            xc = _stage_c(xc, mods, ctx_row, l, *c_weights, tm=n_ctx)
    return x
```

```python
import functools

import jax
import jax.numpy as jnp
from jax import lax
from jax.experimental import pallas as pl
from jax.experimental.pallas import tpu as pltpu

D_MODEL = 1024
DEPTH = 4
GRID_W = 64
N_Q_HEADS = 8
N_KV_HEADS = 2
HEAD_DIM = 64
Q_GROUP = N_Q_HEADS // N_KV_HEADS
WINDOW = 128
BLOCK = 128
ROPE_THETA = 10000.0
POOL_SIZES = (2, 4, 8, 16)
N_POOL_GROUPS = len(POOL_SIZES)
POOL_GROUP_DIM = D_MODEL // 8
POOL_WIDTH = N_POOL_GROUPS * POOL_GROUP_DIM
Q_WIDTH = N_Q_HEADS * HEAD_DIM
KV_WIDTH = N_KV_HEADS * HEAD_DIM
D_FF = 2816
CONV_WIDTH = 3
N_MOD = 6
EPS = 1e-6
NEG_INF = -1e30

LANES = 128
SUBLANES_F32 = 8
SUBLANES_BF16 = 16
VMEM_BYTES_V7X = 64 << 20

KV_POS_WIDTH = 2 * N_KV_HEADS * LANES
PAIR_WIDTH = 2 * HEAD_DIM
A_WIDTH = Q_WIDTH + 2 * KV_WIDTH + POOL_WIDTH
IN_WIDTH = A_WIDTH + 2 * D_MODEL
POOL_HALO = SUBLANES_F32
DOUBLING_MIN_WIDTH = 8
CONV_HALO = SUBLANES_F32
UP_AHEAD = 3
FF_CHUNK = 256
N_FF_CHUNKS = D_FF // FF_CHUNK
MOD_ROWS = 16
TM_A, TQ_B, TM_C = 1024, 512, 512
SUB_A = SUB_B = SUB_C = 256

F32 = jnp.float32
BF16 = jnp.bfloat16


LOG2_E = 1.4426950408889634


def _sigmoid(v):
    return 1.0 / (1.0 + jnp.exp2(v * (-LOG2_E)))


def _modulate(x, g, shift, scale):
    ms = jnp.mean(x * x, axis=-1, keepdims=True)
    y = x * lax.rsqrt(ms + EPS)
    return (y * g) * (1.0 + scale) + shift


def _dot(a, b):
    return jnp.dot(a, b, preferred_element_type=F32)


def _dot_t(a, b):
    return lax.dot_general(a, b, (((1,), (1,)), ((), ())), preferred_element_type=F32)


def _vmem_limit(pipelined_bytes, resident_bytes, scratch_bytes, temp_bytes):
    need = 2 * pipelined_bytes + 2 * resident_bytes + scratch_bytes + temp_bytes
    assert need < VMEM_BYTES_V7X, need
    return int(need)


def _layer_resident(layer, shape):
    return pl.BlockSpec((None,) + tuple(shape), lambda *_: (layer,) + (0,) * len(shape),
                        pipeline_mode=pl.Buffered(1))


def _adaln_kernel(c_ref, w_ref, b_ref, o_ref):
    c = c_ref[...]
    s = c * _sigmoid(c)
    o_ref[0] = jnp.dot(s, w_ref[0], preferred_element_type=F32,
                       precision=lax.Precision.HIGHEST) + b_ref[0]


def _adaln(cond, w_mod, b_mod):
    n_out = N_MOD * D_MODEL
    nc = n_out // 2
    return pl.pallas_call(
        _adaln_kernel,
        grid=(DEPTH, n_out // nc),
        in_specs=[
            pl.BlockSpec((MOD_ROWS, D_MODEL), lambda l, j: (0, 0)),
            pl.BlockSpec((1, D_MODEL, nc), lambda l, j: (l, 0, j)),
            pl.BlockSpec((1, 1, nc), lambda l, j: (l, 0, j)),
        ],
        out_specs=pl.BlockSpec((1, MOD_ROWS, nc), lambda l, j: (l, 0, j)),
        out_shape=jax.ShapeDtypeStruct((DEPTH, MOD_ROWS, n_out), F32),
        compiler_params=pltpu.CompilerParams(
            dimension_semantics=("parallel", "parallel"),
            vmem_limit_bytes=_vmem_limit(D_MODEL * nc * 4 + MOD_ROWS * nc * 8, MOD_ROWS * D_MODEL * 4, 0, 4 << 20)),
        name="adaln",
    )(cond, w_mod, b_mod.reshape(DEPTH, 1, n_out))


def _head_norm_rope(z, gain, rope, out_scale):
    low = lax.broadcasted_iota(jnp.int32, z.shape, 1) < HEAD_DIM
    sq = z * z
    ss_lo = jnp.sum(jnp.where(low, sq, 0.0), axis=-1, keepdims=True)
    ss_hi = jnp.sum(jnp.where(low, 0.0, sq), axis=-1, keepdims=True)
    r_lo = lax.rsqrt(ss_lo * (1.0 / HEAD_DIM) + EPS)
    r_hi = lax.rsqrt(ss_hi * (1.0 / HEAD_DIM) + EPS)
    r = jnp.where(low, r_lo, r_hi)
    t = z * gain
    if rope is not None:
        z_rot, gain_rot, cos, sin = rope
        t = t * cos + (z_rot * gain_rot) * sin
    t = t * r
    if out_scale != 1.0:
        t = t * out_scale
    return t


def _stage_a_kernel(*refs, rope):
    if rope:
        (x_ref, mod_ref, g_ref, w_ref, qg_ref, kg_ref, wr_ref, qgr_ref, kgr_ref, cos_ref, sin_ref,
         q_ref, kz_ref, vz_ref, p_ref) = refs
    else:
        x_ref, mod_ref, g_ref, w_ref, qg_ref, kg_ref, q_ref, kz_ref, vz_ref, p_ref = refs
    d = D_MODEL
    m = mod_ref[0]
    qg, kg = qg_ref[...], kg_ref[...]
    k0 = Q_WIDTH
    p0 = k0 + 2 * KV_WIDTH
    tm = x_ref.shape[1]
    sub = min(tm, SUB_A)

    def project(r):
        h = _modulate(x_ref[0, r, :], g_ref[...], m[:, 0:d], m[:, d:2 * d]).astype(BF16)
        z = {"q": _dot(h, w_ref[:, 0:Q_WIDTH]), "kv": _dot(h, w_ref[:, k0:p0])}
        if rope:
            z["q_rot"] = _dot(h, wr_ref[:, 0:Q_WIDTH])
            z["k_rot"] = _dot(h, wr_ref[:, Q_WIDTH:Q_WIDTH + KV_WIDTH])
        p_ref[0, r, :] = _dot(h, w_ref[:, p0:p0 + POOL_WIDTH])
        return z

    def finish(r, z):
        for c in range(Q_WIDTH // LANES):
            sl = slice(c * LANES, (c + 1) * LANES)
            rp = (z["q_rot"][:, sl], qgr_ref[...], cos_ref[r, :], sin_ref[r, :]) if rope else None
            q_ref[0, r, sl] = _head_norm_rope(z["q"][:, sl], qg, rp, HEAD_DIM ** -0.5).astype(BF16)
        rp = (z["k_rot"], kgr_ref[...], cos_ref[r, :], sin_ref[r, :]) if rope else None
        kk = _head_norm_rope(z["kv"][:, 0:KV_WIDTH], kg, rp, 1.0)
        vv = z["kv"][:, KV_WIDTH:2 * KV_WIDTH]
        low = lax.broadcasted_iota(jnp.int32, kk.shape, 1) < HEAD_DIM
        for src, dst in ((kk, kz_ref), (vv, vz_ref)):
            swapped = pltpu.roll(src, HEAD_DIM, 1)
            dst[0, r, 0 * LANES:1 * LANES] = jnp.where(low, src, 0.0).astype(BF16)
            dst[0, r, 1 * LANES:2 * LANES] = jnp.where(low, 0.0, swapped).astype(BF16)
            dst[0, r, 2 * LANES:3 * LANES] = jnp.where(low, swapped, 0.0).astype(BF16)
            dst[0, r, 3 * LANES:4 * LANES] = jnp.where(low, 0.0, src).astype(BF16)

    slabs = [slice(j * sub, (j + 1) * sub) for j in range(tm // sub)]
    pending = None
    for r in slabs:
        z = project(r)
        if pending is not None:
            finish(*pending)
        pending = (r, z)
    finish(*pending)


def _stage_a(x, mods, mod_row, layer, norm_g, w_in, q_gain, k_gain, rope_args, *, tm):
    bsz, n, d = x.shape
    wa = A_WIDTH
    rope = rope_args is not None
    in_specs = [
        pl.BlockSpec((1, tm, d), lambda i, b: (b, i, 0)),
        pl.BlockSpec((1, 1, N_MOD * d), lambda i, b: (mod_row(b), 0, 0)),
        _layer_resident(layer, (1, d)),
        _layer_resident(layer, (d, wa)),
        _layer_resident(layer, (1, LANES)),
        _layer_resident(layer, (1, LANES)),
    ]
    args = [x, mods, norm_g, w_in, q_gain, k_gain]
    if rope:
        w_rot, q_gain_rot, k_gain_rot, cos, sin = rope_args
        in_specs += [_layer_resident(layer, (d, Q_WIDTH + KV_WIDTH)), _layer_resident(layer, (1, LANES)),
                     _layer_resident(layer, (1, LANES))]
        in_specs += [pl.BlockSpec((tm, LANES), lambda i, b: (i, 0))] * 2
        args += [w_rot, q_gain_rot, k_gain_rot, cos, sin]
    out_shape = [
        jax.ShapeDtypeStruct((bsz, n, Q_WIDTH), BF16),
        jax.ShapeDtypeStruct((bsz, n, KV_POS_WIDTH), BF16),
        jax.ShapeDtypeStruct((bsz, n, KV_POS_WIDTH), BF16),
        jax.ShapeDtypeStruct((bsz, n, POOL_WIDTH), F32),
    ]
    out_specs = [pl.BlockSpec((1, tm, s.shape[2]), lambda i, b: (b, i, 0)) for s in out_shape]
    pipelined = tm * d * 4 + tm * (Q_WIDTH + 2 * KV_POS_WIDTH) * 2 + tm * POOL_WIDTH * 4 + 2 * tm * LANES * 4
    return pl.pallas_call(
        functools.partial(_stage_a_kernel, rope=rope),
        grid=(n // tm, bsz),
        in_specs=in_specs,
        out_specs=out_specs,
        out_shape=out_shape,
        compiler_params=pltpu.CompilerParams(
            dimension_semantics=("parallel", "parallel"),
            vmem_limit_bytes=_vmem_limit(pipelined, d * (wa + Q_WIDTH + KV_WIDTH) * 2, 0, 8 * tm * d * 4)),
        name="stage_a_rope" if rope else "stage_a",
    )(*args)


def _pool_diff(pbuf, tq, seq_len, tile_start):
    pos = tile_start + lax.broadcasted_iota(jnp.int32, (tq, POOL_GROUP_DIM), 0)
    rows = tq + 2 * POOL_HALO

    def window_sums(gi, w):
        cols = slice(gi * POOL_GROUP_DIM, (gi + 1) * POOL_GROUP_DIM)
        if w < DOUBLING_MIN_WIDTH:
            acc = pbuf[pl.ds(POOL_HALO - w // 2, tq), cols]
            for j in range(1, w):
                acc = acc + pbuf[pl.ds(POOL_HALO - w // 2 + j, tq), cols]
            return acc
        s = pbuf[...][:, cols]
        span = 1
        while span < w:
            s = s + pltpu.roll(s, rows - span, 0)
            span *= 2
        start = POOL_HALO - w // 2
        return s[start:start + tq]

    outs = []
    for gi, w in enumerate(POOL_SIZES):
        cols = slice(gi * POOL_GROUP_DIM, (gi + 1) * POOL_GROUP_DIM)
        cnt = (jnp.minimum(pos + (w - w // 2), seq_len) - jnp.maximum(pos - w // 2, 0)).astype(F32)
        outs.append(window_sums(gi, w) / cnt - pbuf[pl.ds(POOL_HALO, tq), cols])
    return outs


def _scores(lhs, pieces):
    out = []
    for idx in (0, 1):
        s = []
        for pc in pieces:
            sc = _dot_t(lhs, pc[idx])
            if pc[4] is not None:
                sc = jnp.where(pc[4], sc, NEG_INF)
            s.append(sc)
        out.append(s)
    return out


def _softmax_pv(scores, pieces, sink_a, sink_b):
    def side(s, sink):
        m = sink
        for sc in s:
            m = jnp.maximum(m, jnp.max(sc, axis=-1, keepdims=True))
        e = [jnp.exp(sc - m) for sc in s]
        den = jnp.sum(e[0], axis=-1, keepdims=True)
        for ee in e[1:]:
            den = den + jnp.sum(ee, axis=-1, keepdims=True)
        den = den + jnp.exp(sink - m)
        return e, den

    e_a, den_a = side(scores[0], sink_a)
    e_b, den_b = side(scores[1], sink_b)
    o = None
    for pc, ea, eb in zip(pieces, e_a, e_b):
        t = _dot(ea.astype(BF16), pc[2]) + _dot(eb.astype(BF16), pc[3])
        o = t if o is None else o + t
    lane = lax.broadcasted_iota(jnp.int32, o.shape, 1)
    return o / jnp.where(lane < HEAD_DIM, den_a, den_b)


def _stage_b_kernel(*refs, tq, seq_len, latent, layer):
    if latent:
        (x_ref, mod_ref, g_ref, q_ref, kp_ref, kc_ref, kn_ref, vp_ref, vc_ref, vn_ref, kx_ref, vx_ref,
         pp_ref, pc_ref, pn_ref, sink_ref, wg_ref, wpool_ref, psc_ref, wba_ref, wbp_ref, wo_ref,
         o_ref, kwin, vwin, pbuf, attn) = refs
    else:
        (x_ref, mod_ref, g_ref, q_ref, kc_ref, vc_ref,
         pp_ref, pc_ref, pn_ref, sink_ref, wg_ref, wpool_ref, psc_ref, wba_ref, wbp_ref, wo_ref,
         o_ref, pbuf, attn) = refs
    d = D_MODEL
    i = pl.program_id(1)
    n_tiles = seq_len // tq
    n_qb = tq // BLOCK
    n_blocks = seq_len // BLOCK

    x = x_ref[0]
    m = mod_ref[0]
    h = _modulate(x, g_ref[...], m[:, 0:d], m[:, d:2 * d]).astype(BF16)

    if latent:
        kwin[0:BLOCK] = kp_ref[0]
        kwin[BLOCK:BLOCK + tq] = kc_ref[0]
        kwin[BLOCK + tq:2 * BLOCK + tq] = kn_ref[0]
        vwin[0:BLOCK] = vp_ref[0]
        vwin[BLOCK:BLOCK + tq] = vc_ref[0]
        vwin[BLOCK + tq:2 * BLOCK + tq] = vn_ref[0]
        row = lax.broadcasted_iota(jnp.int32, (2 * BLOCK, 3 * BLOCK), 0) & (BLOCK - 1)
        col = lax.broadcasted_iota(jnp.int32, (2 * BLOCK, 3 * BLOCK), 1)
        band = (col >= row) & (col <= row + 2 * WINDOW)
    srow = lax.broadcasted_iota(jnp.int32, (2 * BLOCK, 1), 0)

    def unit_pieces(ib, hk):
        r0 = ib * BLOCK
        q0 = hk * Q_GROUP * HEAD_DIM
        lhs = jnp.concatenate([q_ref[0, r0:r0 + BLOCK, q0:q0 + PAIR_WIDTH],
                               q_ref[0, r0:r0 + BLOCK, q0 + PAIR_WIDTH:q0 + 2 * PAIR_WIDTH]], axis=0)
        ca = slice((2 * hk) * LANES, (2 * hk + 1) * LANES)
        cb = slice((2 * hk + 1) * LANES, (2 * hk + 2) * LANES)
        if latent:
            g = i * n_qb + ib
            lo = jnp.where(g > 0, 0, BLOCK)
            hi = jnp.where(g < n_blocks - 1, 3 * BLOCK, 2 * BLOCK)
            mask = band & (col >= lo) & (col < hi)
            rows = slice(r0, r0 + 3 * BLOCK)
            pieces = [(kwin[rows, ca], kwin[rows, cb], vwin[rows, ca], vwin[rows, cb], mask),
                      (kx_ref[0, :, ca], kx_ref[0, :, cb], vx_ref[0, :, ca], vx_ref[0, :, cb], None)]
        else:
            pieces = [(kc_ref[0, :, ca], kc_ref[0, :, cb], vc_ref[0, :, ca], vc_ref[0, :, cb], None)]
        return lhs, pieces

    def unit_scores(ib, hk):
        lhs, pieces = unit_pieces(ib, hk)
        return _scores(lhs, pieces), pieces

    def unit_finish(ib, hk, scores, pieces):
        r0 = ib * BLOCK
        q0 = hk * Q_GROUP * HEAD_DIM
        h0 = hk * Q_GROUP
        sink_a = jnp.where(srow < BLOCK, sink_ref[layer, h0], sink_ref[layer, h0 + 2])
        sink_b = jnp.where(srow < BLOCK, sink_ref[layer, h0 + 1], sink_ref[layer, h0 + 3])
        o = _softmax_pv(scores, pieces, sink_a, sink_b).astype(BF16)
        attn[r0:r0 + BLOCK, q0:q0 + PAIR_WIDTH] = o[0:BLOCK]
        attn[r0:r0 + BLOCK, q0 + PAIR_WIDTH:q0 + 2 * PAIR_WIDTH] = o[BLOCK:2 * BLOCK]

    pbuf[0:POOL_HALO] = jnp.where(i > 0, pp_ref[0], 0.0)
    pbuf[POOL_HALO:POOL_HALO + tq] = pc_ref[0]
    pbuf[POOL_HALO + tq:2 * POOL_HALO + tq] = jnp.where(i < n_tiles - 1, pn_ref[0], 0.0)
    pool_d = [dg.astype(BF16) for dg in _pool_diff(pbuf, tq, seq_len, i * tq)]

    n_sub = tq // SUB_B
    qb_per_sub = SUB_B // BLOCK

    def slab(j):
        return slice(j * SUB_B, (j + 1) * SUB_B)

    def gate(j, col0):
        return _sigmoid(_dot(h[slab(j)], wg_ref[:, col0:col0 + d]))

    def pool_branch(j):
        pooled = []
        for gi, dg in enumerate(pool_d):
            cols = slice(gi * POOL_GROUP_DIM, (gi + 1) * POOL_GROUP_DIM)
            pooled.append(_dot(dg[slab(j)], wpool_ref[gi]) * psc_ref[:, cols])
        return _dot(jnp.concatenate(pooled, axis=-1).astype(BF16), wbp_ref[...])

    def merge(j, gate_attn, gate_pool, z_pool):
        r = slab(j)
        y = gate_attn * _dot(attn[r, :], wba_ref[...]) + gate_pool * z_pool
        o_ref[0, r, :] = x[r] + m[:, 2 * d:3 * d] * _dot(y.astype(BF16), wo_ref[...])

    units = [(ib, hk) for ib in range(n_qb) for hk in range(N_KV_HEADS)]
    units_per_sub = qb_per_sub * N_KV_HEADS
    dense = []
    for j in range(n_sub):
        dense += [functools.partial(gate, j, A_WIDTH), functools.partial(gate, j, A_WIDTH + d),
                  functools.partial(pool_branch, j)]
    assert units_per_sub >= 3
    dense_out = []
    pending = unit_scores(*units[0])
    for k, u in enumerate(units):
        nxt = unit_scores(*units[k + 1]) if k + 1 < len(units) else None
        if k < len(dense):
            dense_out.append(dense[k]())
        unit_finish(*u, *pending)
        pending = nxt
        if (k + 1) % units_per_sub == 0:
            j = k // units_per_sub
            assert len(dense_out) >= 3 * (j + 1)
            merge(j, *dense_out[3 * j:3 * j + 3])


def _stage_b(x, mods, mod_row, layer, norm_g, q, kz, vz, ctx_kv, p, sink, w_in, w_pool, pool_scale,
             w_br_attn, w_br_pool, w_out, *, tq):
    bsz, n, d = x.shape
    latent = ctx_kv is not None
    n_qb = tq // BLOCK
    nb = n // BLOCK
    tile = lambda b, i: (b, i, 0)
    in_specs = [
        pl.BlockSpec((1, tq, d), tile),
        pl.BlockSpec((1, 1, N_MOD * d), lambda b, i: (mod_row(b), 0, 0)),
        _layer_resident(layer, (1, d)),
        pl.BlockSpec((1, tq, Q_WIDTH), tile),
    ]
    args = [x, mods, norm_g, q]
    kv_cur = pl.BlockSpec((1, tq, KV_POS_WIDTH), tile)
    if latent:
        kv_prev = pl.BlockSpec((1, BLOCK, KV_POS_WIDTH), lambda b, i: (b, jnp.maximum(i * n_qb - 1, 0), 0))
        kv_next = pl.BlockSpec((1, BLOCK, KV_POS_WIDTH), lambda b, i: (b, jnp.minimum((i + 1) * n_qb, nb - 1), 0))
        n_ctx = ctx_kv[0].shape[1]
        kv_ctx = pl.BlockSpec((1, n_ctx, KV_POS_WIDTH), lambda b, i: (b, 0, 0))
        in_specs += [kv_prev, kv_cur, kv_next, kv_prev, kv_cur, kv_next, kv_ctx, kv_ctx]
        args += [kz, kz, kz, vz, vz, vz, ctx_kv[0], ctx_kv[1]]
    else:
        in_specs += [kv_cur, kv_cur]
        args += [kz, vz]
    n_halo = n // POOL_HALO
    per_tile = tq // POOL_HALO
    in_specs += [
        pl.BlockSpec((1, POOL_HALO, POOL_WIDTH), lambda b, i: (b, jnp.maximum(i * per_tile - 1, 0), 0)),
        pl.BlockSpec((1, tq, POOL_WIDTH), tile),
        pl.BlockSpec((1, POOL_HALO, POOL_WIDTH), lambda b, i: (b, jnp.minimum((i + 1) * per_tile, n_halo - 1), 0)),
        pl.BlockSpec(memory_space=pltpu.SMEM),
        _layer_resident(layer, (d, IN_WIDTH)),
        _layer_resident(layer, (N_POOL_GROUPS, POOL_GROUP_DIM, POOL_GROUP_DIM)),
        _layer_resident(layer, (1, POOL_WIDTH)),
        _layer_resident(layer, (Q_WIDTH, d)),
        _layer_resident(layer, (POOL_WIDTH, d)),
        _layer_resident(layer, (d, d)),
    ]
    args += [p, p, p, sink, w_in, w_pool, pool_scale, w_br_attn, w_br_pool, w_out]
    scratch = []
    scratch_bytes = 0
    if latent:
        scratch += [pltpu.VMEM((tq + 2 * BLOCK, KV_POS_WIDTH), BF16)] * 2
        scratch_bytes += 2 * (tq + 2 * BLOCK) * KV_POS_WIDTH * 2
    scratch += [pltpu.VMEM((tq + 2 * POOL_HALO, POOL_WIDTH), F32), pltpu.VMEM((tq, Q_WIDTH), BF16)]
    scratch_bytes += (tq + 2 * POOL_HALO) * POOL_WIDTH * 4 + tq * Q_WIDTH * 2
    pipelined = (2 * tq * d * 4 + tq * Q_WIDTH * 2 + 2 * (tq + 3 * BLOCK) * KV_POS_WIDTH * 2
                 + (tq + 2 * POOL_HALO) * POOL_WIDTH * 4)
    resident = (IN_WIDTH * d + Q_WIDTH * d + POOL_WIDTH * d + d * d) * 2
    return pl.pallas_call(
        functools.partial(_stage_b_kernel, tq=tq, seq_len=n, latent=latent, layer=layer),
        grid=(bsz, n // tq),
        in_specs=in_specs,
        out_specs=pl.BlockSpec((1, tq, d), tile),
        out_shape=jax.ShapeDtypeStruct((bsz, n, d), F32),
        scratch_shapes=scratch,
        compiler_params=pltpu.CompilerParams(
            dimension_semantics=("parallel", "parallel"),
            vmem_limit_bytes=_vmem_limit(pipelined, resident, scratch_bytes, 12 * SUB_B * d * 4 + 4 * tq * d * 4)),
        name="stage_b_latent" if latent else "stage_b_ctx",
    )(*args)


def _stage_c_kernel(x_ref, xp_ref, xn_ref, mod_ref, g_ref, wu_ref, cw_ref, cb_ref, wd_ref, o_ref,
                    hbuf, *, tm, seq_len):
    d = D_MODEL
    i = pl.program_id(1)
    n_tiles = seq_len // tm
    n_sub = tm // SUB_C
    m = mod_ref[0]
    g = g_ref[...]
    shift, scale = m[:, 3 * d:4 * d], m[:, 4 * d:5 * d]
    x = x_ref[0]
    hp = jnp.where(i > 0, _modulate(xp_ref[0], g, shift, scale), 0.0)
    hn = jnp.where(i < n_tiles - 1, _modulate(xn_ref[0], g, shift, scale), 0.0)
    h = _modulate(x, g, shift, scale)
    rows = SUB_C + 2 * CONV_HALO
    assert SUB_C % SUBLANES_BF16 == 0 and (2 * CONV_HALO) % SUBLANES_BF16 == 0
    for j in range(n_sub):
        r0 = j * SUB_C
        nxt = hn if j == n_sub - 1 else h[r0 + SUB_C:r0 + SUB_C + CONV_HALO]
        prv = hp if j == 0 else h[r0 - CONV_HALO:r0]
        hbuf[j, 0:SUB_C] = h[r0:r0 + SUB_C].astype(BF16)
        hbuf[j, SUB_C:rows] = jnp.concatenate([nxt, prv], axis=0).astype(BF16)

    def conv(u, cols):
        uc = cb_ref[:, cols] + pltpu.roll(u, 1, 0)[0:SUB_C] * cw_ref[0:1, cols]
        uc = uc + u[0:SUB_C] * cw_ref[1:2, cols]
        return uc + pltpu.roll(u, rows - 1, 0)[0:SUB_C] * cw_ref[2:3, cols]

    def conv_act(u, c):
        a = conv(u[0], slice(c * FF_CHUNK, (c + 1) * FF_CHUNK))
        b = conv(u[1], slice(D_FF + c * FF_CHUNK, D_FF + (c + 1) * FF_CHUNK))
        return ((a * _sigmoid(a)) * b).astype(BF16)

    def up(j, c):
        return [_dot(hbuf[j], wu_ref[:, off + c * FF_CHUNK:off + (c + 1) * FF_CHUNK]) for off in (0, D_FF)]

    acc = [None] * n_sub
    act_prev = [None] * n_sub
    u_queue = [[up(j, c) for c in range(UP_AHEAD)] for j in range(n_sub)]
    for c in range(N_FF_CHUNKS + 1):
        for j in range(n_sub):
            if c + UP_AHEAD < N_FF_CHUNKS:
                u_queue[j].append(up(j, c + UP_AHEAD))
            act = conv_act(u_queue[j].pop(0), c) if c < N_FF_CHUNKS else None
            if c >= 1:
                krows = slice((c - 1) * FF_CHUNK, c * FF_CHUNK)
                t = [_dot(act_prev[j], wd_ref[krows, k * (d // 2):(k + 1) * (d // 2)]) for k in range(2)]
                acc[j] = t if acc[j] is None else [a + b for a, b in zip(acc[j], t)]
            act_prev[j] = act
    for j in range(n_sub):
        r = slice(j * SUB_C, (j + 1) * SUB_C)
        o_ref[0, r, :] = x[r] + m[:, 5 * d:6 * d] * jnp.concatenate(acc[j], axis=-1)


def _stage_c(x, mods, mod_row, layer, norm_g, w_up, conv_w, conv_b, w_down, *, tm):
    bsz, n, d = x.shape
    tile = lambda b, i: (b, i, 0)
    per_tile = tm // CONV_HALO
    n_halo = n // CONV_HALO
    in_specs = [
        pl.BlockSpec((1, tm, d), tile),
        pl.BlockSpec((1, CONV_HALO, d), lambda b, i: (b, jnp.maximum(i * per_tile - 1, 0), 0)),
        pl.BlockSpec((1, CONV_HALO, d), lambda b, i: (b, jnp.minimum((i + 1) * per_tile, n_halo - 1), 0)),
        pl.BlockSpec((1, 1, N_MOD * d), lambda b, i: (mod_row(b), 0, 0)),
        _layer_resident(layer, (1, d)),
        _layer_resident(layer, (d, 2 * D_FF)),
        _layer_resident(layer, (CONV_WIDTH, 2 * D_FF)),
        _layer_resident(layer, (1, 2 * D_FF)),
        _layer_resident(layer, (D_FF, d)),
    ]
    assert tm % SUB_C == 0
    hbuf_shape = (tm // SUB_C, SUB_C + 2 * CONV_HALO, d)
    scratch_bytes = hbuf_shape[0] * hbuf_shape[1] * d * 2
    return pl.pallas_call(
        functools.partial(_stage_c_kernel, tm=tm, seq_len=n),
        grid=(bsz, n // tm),
        in_specs=in_specs,
        out_specs=pl.BlockSpec((1, tm, d), tile),
        out_shape=jax.ShapeDtypeStruct((bsz, n, d), F32),
        scratch_shapes=[pltpu.VMEM(hbuf_shape, BF16)],
        compiler_params=pltpu.CompilerParams(
            dimension_semantics=("parallel", "parallel"),
            vmem_limit_bytes=_vmem_limit(2 * tm * d * 4 + 2 * CONV_HALO * d * 4, 3 * d * D_FF * 2,
                                         scratch_bytes, 8 * tm * d * 4)),
        name="stage_c",
    )(x, x, x, mods, norm_g, w_up, conv_w, conv_b, w_down)


def _rope_tables(n_tok):
    rows = n_tok // GRID_W
    row = jnp.repeat(jnp.arange(rows, dtype=jnp.int32), GRID_W).astype(F32)
    col = jnp.tile(jnp.arange(GRID_W, dtype=jnp.int32), rows).astype(F32)
    n_freq = HEAD_DIM // 4
    inv = ROPE_THETA ** (-jnp.arange(n_freq, dtype=F32) / n_freq)
    ar, ac = row[:, None] * inv, col[:, None] * inv
    cos = jnp.concatenate([jnp.cos(ar), jnp.cos(ar), jnp.cos(ac), jnp.cos(ac)], axis=-1)
    sin = jnp.concatenate([-jnp.sin(ar), jnp.sin(ar), -jnp.sin(ac), jnp.sin(ac)], axis=-1)
    return jnp.tile(cos, (1, LANES // HEAD_DIM)), jnp.tile(sin, (1, LANES // HEAD_DIM))


def _swap_rope_halves(a):
    quarter = HEAD_DIM // 4
    upper = (jnp.arange(a.shape[-1]) & quarter) != 0
    return jnp.where(upper, jnp.roll(a, quarter, axis=-1), jnp.roll(a, -quarter, axis=-1))


def kernel(x, c, ctx, c_ctx, w_mod, b_mod, norm1_g, norm2_g, w_in, q_gain, k_gain, sink, w_pool, pool_scale,
           w_br_attn, w_br_pool, w_out, w_up, conv_w, conv_b, w_down):
    bsz, n_tok, d = x.shape
    n_ctx = ctx.shape[1]
    assert bsz + 1 <= MOD_ROWS
    cond = jnp.zeros((MOD_ROWS, d), F32).at[:bsz].set(c).at[bsz].set(c_ctx)
    mods = _adaln(cond, w_mod, b_mod).reshape(DEPTH * MOD_ROWS, 1, N_MOD * d)

    w_in_b = w_in.astype(BF16)
    g1 = norm1_g.reshape(DEPTH, 1, d)
    g2 = norm2_g.reshape(DEPTH, 1, d)
    qg = jnp.tile(q_gain, (1, LANES // HEAD_DIM)).reshape(DEPTH, 1, LANES)
    kg = jnp.tile(k_gain, (1, LANES // HEAD_DIM)).reshape(DEPTH, 1, LANES)
    rope_args = (_swap_rope_halves(w_in_b[:, :, :Q_WIDTH + KV_WIDTH]),
                 _swap_rope_halves(qg), _swap_rope_halves(kg)) + _rope_tables(n_tok)
    b_weights = (sink, w_in_b, w_pool.astype(BF16), pool_scale.reshape(DEPTH, 1, POOL_WIDTH),
                 w_br_attn.astype(BF16), w_br_pool.astype(BF16), w_out.astype(BF16))
    c_weights = (g2, w_up.astype(BF16), conv_w, conv_b.reshape(DEPTH, 1, 2 * D_FF), w_down.astype(BF16))

    xc = ctx
    for l in range(DEPTH):
        last = l == DEPTH - 1
        lat_row = lambda b, l=l: l * MOD_ROWS + b
        ctx_row = lambda b, l=l: l * MOD_ROWS + bsz
        qc, kzc, vzc, pc = _stage_a(xc, mods, ctx_row, l, g1, w_in_b, qg, kg, None, tm=n_ctx)
        q, kz, vz, p = _stage_a(x, mods, lat_row, l, g1, w_in_b, qg, kg, rope_args, tm=TM_A)
        x = _stage_b(x, mods, lat_row, l, g1, q, kz, vz, (kzc, vzc), p, *b_weights, tq=TQ_B)
        x = _stage_c(x, mods, lat_row, l, *c_weights, tm=TM_C)
        if not last:
            xc = _stage_b(xc, mods, ctx_row, l, g1, qc, kzc, vzc, None, pc, *b_weights, tq=n_ctx)
            xc = _stage_c(xc, mods, ctx_row, l, *c_weights, tm=n_ctx)
    return x
```
